```python
import math
import numpy as np
import jax
import jax.numpy as jnp
from jax import lax

D_MODEL = 1024
BATCH = 16
SEQ = 2048
DEPTH = 2
DEC_BATCH = 8
DEC_SEQ = 64
PAST_LEN = 2048

CHUNK = 64
Q_BLOCK = 128
EPS = 1e-6
NEG_INF = -1e30

H_A = 4
DK_A = 32
DV_A = 64
GLA_RANK = 16
GLA_TAU = 16.0
H_D = 4
DK_D = 64
DV_D = 128
H_R = 4
DK_R = 64
DV_R = 64
ROPE_BASE = 10000.0
D_FF = 2816

W_GLA = H_A * DV_A
W_DIFF = H_D * DV_D
W_RET = H_R * DV_R
D_MIX = W_GLA + W_DIFF + W_RET
IN_SPLITS = (GLA_RANK, H_A * DK_A, H_A * DK_A, W_GLA, W_GLA,
             2 * H_D * DK_D, 2 * H_D * DK_D, W_DIFF,
             H_R * DK_R, H_R * DK_R, W_RET, W_RET)
D_IN = 3344

kernel_name = 'hymba_gla_diff_retention_macaron_step'


def _rms_norm(x, g):
    xf = x.astype(jnp.float32)
    y = xf * lax.rsqrt(jnp.mean(xf * xf, axis=-1, keepdims=True) + EPS)
    return (y * g.astype(jnp.float32)).astype(x.dtype)


def _layer_norm(x, g):
    xf = x.astype(jnp.float32)
    xc = xf - jnp.mean(xf, axis=-1, keepdims=True)
    y = xc * lax.rsqrt(jnp.mean(xc * xc, axis=-1, keepdims=True) + EPS)
    return (y * g.astype(jnp.float32)).astype(x.dtype)


def _swiglu(h, w_up, w_down):
    gate, up = jnp.split(h @ w_up, 2, axis=-1)
    return (jax.nn.silu(gate) * up) @ w_down


def _to_chunks(x, c):
    b, t, h, d = x.shape
    return x.reshape(b, t // c, c, h, d).transpose(1, 0, 3, 2, 4)


def _from_chunks(o):
    n, b, h, c, d = o.shape
    return o.transpose(1, 0, 3, 2, 4).reshape(b, n * c, h, d)


def _rotary(x, pos):
    half = x.shape[-1] // 2
    inv = ROPE_BASE ** (-jnp.arange(half, dtype=jnp.float32) / half)
    ang = pos.astype(jnp.float32)[:, None] * inv[None, :]
    cos = jnp.cos(ang)[None, :, None, :]
    sin = jnp.sin(ang)[None, :, None, :]
    xf = x.astype(jnp.float32)
    x1, x2 = xf[..., :half], xf[..., half:]
    return jnp.concatenate([x1 * cos - x2 * sin, x1 * sin + x2 * cos], axis=-1).astype(x.dtype)


def _gla_chunked(q, k, v, log_a, s0):
    t = q.shape[1]
    c = min(CHUNK, t)
    tri = jnp.tril(jnp.ones((c, c), dtype=bool))

    def step(s, inp):
        qc, kc, vc, ac = inp
        b = jnp.cumsum(ac, axis=2)
        rel = jnp.where(tri[:, :, None], b[:, :, :, None, :] - b[:, :, None, :, :], -jnp.inf)
        scores = jnp.einsum('bhtsd,bhsd->bhts', qc[:, :, :, None, :] * jnp.exp(rel), kc)
        o = jnp.einsum('bhts,bhsv->bhtv', scores, vc) + jnp.einsum('bhtd,bhdv->bhtv', qc * jnp.exp(b), s)
        b_last = b[:, :, -1]
        s_new = jnp.exp(b_last)[..., None] * s + jnp.einsum(
            'bhsd,bhsv->bhdv', kc * jnp.exp(b_last[:, :, None, :] - b), vc)
        return s_new, o

    xs = tuple(_to_chunks(a.astype(jnp.float32), c) for a in (q, k, v, log_a))
    s_fin, o = lax.scan(step, s0.astype(jnp.float32), xs)
    return _from_chunks(o), s_fin.astype(s0.dtype)


def _retention_chunked(q, k, v, s0):
    t = q.shape[1]
    c = min(CHUNK, t)
    lg = jnp.log(1.0 - 2.0 ** (-5.0 - jnp.arange(H_R, dtype=jnp.float32)))
    idx = jnp.arange(c, dtype=jnp.float32)
    tri = idx[:, None] >= idx[None, :]
    dec = jnp.exp(jnp.where(tri[None], (idx[:, None] - idx[None, :])[None] * lg[:, None, None], -jnp.inf))
    q_dec = jnp.exp((idx[None, :] + 1.0) * lg[:, None])
    k_dec = jnp.exp((c - 1.0 - idx[None, :]) * lg[:, None])
    c_dec = jnp.exp(c * lg)

    def step(s, inp):
        qc, kc, vc = inp
        scores = jnp.einsum('bhtd,bhsd->bhts', qc, kc) * dec[None]
        o = jnp.einsum('bhts,bhsv->bhtv', scores, vc) + jnp.einsum(
            'bhtd,bhdv->bhtv', qc * q_dec[None, :, :, None], s)
        s_new = c_dec[None, :, None, None] * s + jnp.einsum(
            'bhsd,bhsv->bhdv', kc * k_dec[None, :, :, None], vc)
        return s_new, o

    xs = tuple(_to_chunks(a.astype(jnp.float32), c) for a in (q, k, v))
    s_fin, o = lax.scan(step, s0.astype(jnp.float32), xs)
    return _from_chunks(o), s_fin.astype(s0.dtype)


def _diff_attention(q, k, v, q_pos, k_pos, lam):
    b, tq = q.shape[0], q.shape[1]
    qb = min(Q_BLOCK, tq)
    nb = tq // qb
    kf = k.astype(jnp.float32) * DK_D ** -0.5
    vf = v.astype(jnp.float32)
    slopes = 2.0 ** (-8.0 * jnp.arange(1, H_D + 1, dtype=jnp.float32) / H_D)
    k_chunk = k_pos // CHUNK
    q_blocks = q.astype(jnp.float32).reshape(b, nb, qb, H_D, 2, DK_D).swapaxes(0, 1)
    p_blocks = q_pos.reshape(nb, qb)

    def one_block(args):
        qblk, pblk = args
        s = jnp.einsum('bqhmd,bkhmd->bmhqk', qblk, kf)
        dist = jnp.abs(pblk[:, None] - k_pos[None, :]).astype(jnp.float32)
        visible = k_chunk[None, :] <= (pblk // CHUNK)[:, None]
        s = jnp.where(visible, s - slopes[:, None, None] * dist, NEG_INF)
        p = jax.nn.softmax(s, axis=-1)
        a = p[:, 0] - lam * p[:, 1]
        return jnp.einsum('bhqk,bkhv->bqhv', a, vf)

    o = lax.map(one_block, (q_blocks, p_blocks))
    return o.swapaxes(0, 1).reshape(b, tq, H_D, DV_D)


def _mixer(h, l, P, past_k, past_v, s_gla0, s_ret0):
    b, t, _ = h.shape
    past_len = 0 if past_k is None else past_k.shape[1]
    pos = past_len + jnp.arange(t)
    z = h @ P['w_in'][l]
    offs = [int(o) for o in np.cumsum(IN_SPLITS)[:-1]]
    a_lr, gq, gk, gv, gr, dq, dk, dv, rq, rk, rv, rg = jnp.split(z, offs, axis=-1)

    q_a = gq.reshape(b, t, H_A, DK_A) * DK_A ** -0.5
    k_a = gk.reshape(b, t, H_A, DK_A)
    v_a = gv.reshape(b, t, H_A, DV_A)
    log_a = jax.nn.log_sigmoid((a_lr @ P['w_gla_a2'][l] + P['b_gla_a'][l]).astype(jnp.float32)) / GLA_TAU
    o_a, s_gla = _gla_chunked(q_a, k_a, v_a, log_a.reshape(b, t, H_A, DK_A), s_gla0)
    o_a = _rms_norm(o_a.astype(h.dtype), P['gla_norm'][l]).reshape(b, t, W_GLA) * jax.nn.silu(gr)

    q_d = dq.reshape(b, t, H_D, 2, DK_D)
    k_rows = dk.reshape(b, t, H_D, 2 * DK_D)
    v_rows = dv.reshape(b, t, H_D, DV_D)
    if past_k is None:
        k_all, v_all = k_rows, v_rows
    else:
        k_all = jnp.concatenate([past_k.astype(k_rows.dtype), k_rows], axis=1)
        v_all = jnp.concatenate([past_v.astype(v_rows.dtype), v_rows], axis=1)
    lam_p = P['diff_lambda'][l].astype(jnp.float32)
    lam_init = 0.8 - 0.6 * math.exp(-0.3 * l)
    lam = jnp.exp(jnp.sum(lam_p[0] * lam_p[1])) - jnp.exp(jnp.sum(lam_p[2] * lam_p[3])) + lam_init
    o_d = _diff_attention(q_d, k_all.reshape(b, past_len + t, H_D, 2, DK_D), v_all,
                          pos, jnp.arange(past_len + t), lam)
    o_d = (_rms_norm(o_d.astype(h.dtype), P['diff_subln'][l]) * (1.0 - lam_init)).reshape(b, t, W_DIFF)

    q_r = _rotary(rq.reshape(b, t, H_R, DK_R), pos)
    k_r = _rotary(rk.reshape(b, t, H_R, DK_R), pos) * DK_R ** -0.5
    v_r = rv.reshape(b, t, H_R, DV_R)
    o_r, s_ret = _retention_chunked(q_r, k_r, v_r, s_ret0)
    o_r = _layer_norm(o_r.astype(h.dtype), P['ret_norm'][l]).reshape(b, t, W_RET) * jax.nn.silu(rg)

    out = jnp.concatenate([o_a, o_d, o_r], axis=-1) @ P['w_out'][l]
    return out, k_rows, v_rows, s_gla, s_ret


def _layer(x, l, P, past_k, past_v, s_gla0, s_ret0):
    h = _rms_norm(x, P['ffn1_norm_pre'][l])
    x = x + 0.5 * _rms_norm(_swiglu(h, P['ffn1_w_up'][l], P['ffn1_w_down'][l]), P['ffn1_norm_post'][l])
    h = _rms_norm(x, P['mix_norm_pre'][l])
    m, k_rows, v_rows, s_gla, s_ret = _mixer(h, l, P, past_k, past_v, s_gla0, s_ret0)
    x = x + _rms_norm(m, P['mix_norm_post'][l])
    h = _rms_norm(x, P['ffn2_norm_pre'][l])
    x = x + 0.5 * _rms_norm(_swiglu(h, P['ffn2_w_up'][l], P['ffn2_w_down'][l]), P['ffn2_norm_post'][l])
    return x, k_rows, v_rows, s_gla, s_ret


def setup_inputs(seed: int = 0) -> dict:
    key = jax.random.key(seed)
    ks = jax.random.split(key, 24)
    f32 = jnp.float32

    def nrm(k, shape, scale):
        return jax.random.normal(k, shape, f32) * scale

    def gain(k, shape):
        return 1.0 + 0.05 * jax.random.normal(k, shape, f32)

    return {
        'x_prompt': nrm(ks[0], (BATCH, SEQ, D_MODEL), 1.0),
        'x_sample': nrm(ks[1], (DEC_BATCH, DEC_SEQ, D_MODEL), 1.0),
        'cache_diff_k': nrm(ks[2], (DEPTH, DEC_BATCH, PAST_LEN, H_D, 2 * DK_D), 1.0),
        'cache_diff_v': nrm(ks[3], (DEPTH, DEC_BATCH, PAST_LEN, H_D, DV_D), 1.0),
        'state_gla': nrm(ks[4], (DEPTH, DEC_BATCH, H_A, DK_A, DV_A), 1.0),
        'state_ret': nrm(ks[5], (DEPTH, DEC_BATCH, H_R, DK_R, DV_R), 1.0),
        'ffn1_norm_pre': gain(ks[6], (DEPTH, D_MODEL)),
        'ffn1_w_up': nrm(ks[7], (DEPTH, D_MODEL, 2 * D_FF), D_MODEL ** -0.5),
        'ffn1_w_down': nrm(ks[8], (DEPTH, D_FF, D_MODEL), D_FF ** -0.5),
        'ffn1_norm_post': gain(ks[9], (DEPTH, D_MODEL)),
        'mix_norm_pre': gain(ks[10], (DEPTH, D_MODEL)),
        'w_in': nrm(ks[11], (DEPTH, D_MODEL, D_IN), D_MODEL ** -0.5),
        'w_gla_a2': nrm(ks[12], (DEPTH, GLA_RANK, H_A * DK_A), GLA_RANK ** -0.5),
        'b_gla_a': nrm(ks[13], (DEPTH, H_A * DK_A), 0.1),
        'gla_norm': gain(ks[14], (DEPTH, H_A, DV_A)),
        'diff_lambda': nrm(ks[15], (DEPTH, 4, DK_D), 0.1),
        'diff_subln': gain(ks[16], (DEPTH, DV_D)),
        'ret_norm': gain(ks[17], (DEPTH, H_R, DV_R)),
        'w_out': nrm(ks[18], (DEPTH, D_MIX, D_MODEL), D_MIX ** -0.5),
        'mix_norm_post': gain(ks[19], (DEPTH, D_MODEL)),
        'ffn2_norm_pre': gain(ks[20], (DEPTH, D_MODEL)),
        'ffn2_w_up': nrm(ks[21], (DEPTH, D_MODEL, 2 * D_FF), D_MODEL ** -0.5),
        'ffn2_w_down': nrm(ks[22], (DEPTH, D_FF, D_MODEL), D_FF ** -0.5),
        'ffn2_norm_post': gain(ks[23], (DEPTH, D_MODEL)),
    }


def reference(x_prompt, x_sample, cache_diff_k, cache_diff_v, state_gla, state_ret,
              ffn1_norm_pre, ffn1_w_up, ffn1_w_down, ffn1_norm_post,
              mix_norm_pre, w_in, w_gla_a2, b_gla_a, gla_norm, diff_lambda, diff_subln,
              ret_norm, w_out, mix_norm_post,
              ffn2_norm_pre, ffn2_w_up, ffn2_w_down, ffn2_norm_post):
    P = {
        'ffn1_norm_pre': ffn1_norm_pre, 'ffn1_w_up': ffn1_w_up, 'ffn1_w_down': ffn1_w_down,
        'ffn1_norm_post': ffn1_norm_post, 'mix_norm_pre': mix_norm_pre, 'w_in': w_in,
        'w_gla_a2': w_gla_a2, 'b_gla_a': b_gla_a, 'gla_norm': gla_norm, 'diff_lambda': diff_lambda,
        'diff_subln': diff_subln, 'ret_norm': ret_norm, 'w_out': w_out, 'mix_norm_post': mix_norm_post,
        'ffn2_norm_pre': ffn2_norm_pre, 'ffn2_w_up': ffn2_w_up, 'ffn2_w_down': ffn2_w_down,
        'ffn2_norm_post': ffn2_norm_post,
    }
    bp = x_prompt.shape[0]
    xp, xs = x_prompt, x_sample
    kp, vp, gp, rp = [], [], [], []
    ksm, vsm, gsm, rsm = [], [], [], []
    for l in range(DEPTH):
        xp, k_, v_, g_, r_ = _layer(xp, l, P, None, None,
                                    jnp.zeros((bp, H_A, DK_A, DV_A), x_prompt.dtype),
                                    jnp.zeros((bp, H_R, DK_R, DV_R), x_prompt.dtype))
        kp.append(k_); vp.append(v_); gp.append(g_); rp.append(r_)
        xs, k_, v_, g_, r_ = _layer(xs, l, P, cache_diff_k[l], cache_diff_v[l], state_gla[l], state_ret[l])
        ksm.append(k_); vsm.append(v_); gsm.append(g_); rsm.append(r_)
    new_k_prompt = jnp.stack(kp)
    new_v_prompt = jnp.stack(vp)
    state_gla_prompt = jnp.stack(gp)
    state_ret_prompt = jnp.stack(rp)
    new_k_sample = jnp.stack(ksm)
    new_v_sample = jnp.stack(vsm)
    state_gla_sample = jnp.stack(gsm)
    state_ret_sample = jnp.stack(rsm)
    return (xp, xs, new_k_prompt, new_v_prompt, state_gla_prompt, state_ret_prompt,
            new_k_sample, new_v_sample, state_gla_sample, state_ret_sample)
```

```python
import functools
import math

import jax
import jax.numpy as jnp
from jax import lax
from jax.experimental import pallas as pl
from jax.experimental.pallas import tpu as pltpu

F32 = jnp.float32
BF16 = jnp.bfloat16

D_MODEL = 1024
D_FF = 2816
EPS = 1e-6
NEG_INF = -1e30
CHUNK = 64

H_A, DK_A, DV_A = 4, 32, 64
GLA_RANK = 16
GLA_TAU = 16.0
H_D, DK_D, DV_D = 4, 64, 128
H_R, DK_R, DV_R = 4, 64, 64
ROPE_BASE = 10000.0

W_GLA = H_A * DV_A
W_DIFF = H_D * DV_D
W_RET = H_R * DV_R

LANES = 128
SUBLANES = 8

Z_GV, Z_GR, Z_RQ, Z_RK, Z_RV, Z_RG = 0, 256, 512, 768, 1024, 1280
Z_DQ, Z_GQ, Z_GK, Z_LA = 1536, 2048, 2176, 2304
Z_W = 2432
WP_ALR = 2304
WP_DK = 2432
WP_DV = 2944
WP_W = 3456

TOKEN_TILE = 512
FF_COLS = 256
SCAN_TILE = 512
RET_CHUNK = 128
ATTN_TILE = 256
ATTN_CTX_TILE = 512

_MIB = 1024 * 1024


def _cparams(sem, vmem_mib):
    return pltpu.CompilerParams(dimension_semantics=sem, vmem_limit_bytes=vmem_mib * _MIB)


def _rms(x, g):
    return x * lax.rsqrt(jnp.mean(x * x, axis=-1, keepdims=True) + EPS) * g


def _sigmoid(x):
    return 1.0 / (1.0 + jnp.exp(-x))


def _const_spec(shape):
    n = len(shape)
    return pl.BlockSpec(shape, lambda *_: (0,) * n, pipeline_mode=pl.Buffered(1))


def _ffn_kernel(x_ref, gpre_ref, wup_ref, wdown_ref, gpost_ref, o_ref, h_scr, act_scr):
    x = x_ref[...]
    h_scr[...] = _rms(x, gpre_ref[...]).astype(BF16)
    for c in range(D_FF // FF_COLS):
        lo = c * FF_COLS
        gate = jnp.dot(h_scr[...], wup_ref[:, lo:lo + FF_COLS], preferred_element_type=F32)
        up = jnp.dot(h_scr[...], wup_ref[:, D_FF + lo:D_FF + lo + FF_COLS], preferred_element_type=F32)
        act_scr[:, lo:lo + FF_COLS] = (gate * _sigmoid(gate) * up).astype(BF16)
    y = jnp.dot(act_scr[...], wdown_ref[...], preferred_element_type=F32)
    o_ref[...] = x + 0.5 * _rms(y, gpost_ref[...])


def _ffn(x, gpre, wup, wdown, gpost):
    n = x.shape[0]
    tm = min(TOKEN_TILE, n)
    return pl.pallas_call(
        _ffn_kernel,
        grid=(n // tm,),
        in_specs=[
            pl.BlockSpec((tm, D_MODEL), lambda i: (i, 0)),
            _const_spec((1, D_MODEL)),
            _const_spec((D_MODEL, 2 * D_FF)),
            _const_spec((D_FF, D_MODEL)),
            _const_spec((1, D_MODEL)),
        ],
        out_specs=pl.BlockSpec((tm, D_MODEL), lambda i: (i, 0)),
        out_shape=jax.ShapeDtypeStruct((n, D_MODEL), F32),
        scratch_shapes=[pltpu.VMEM((tm, D_MODEL), BF16), pltpu.VMEM((tm, D_FF), BF16)],
        compiler_params=_cparams(("parallel",), 48),
        name="ffn",
    )(x, gpre, wup, wdown, gpost)


def _inproj_kernel(x_ref, g_ref, w_ref, wa2_ref, ba_ref, z_ref, k_ref, v_ref, h_scr):
    h_scr[...] = _rms(x_ref[...], g_ref[...]).astype(BF16)
    for lo in range(0, WP_ALR, 256):
        z_ref[:, lo:lo + 256] = jnp.dot(h_scr[...], w_ref[:, lo:lo + 256], preferred_element_type=F32)
    a_lr = jnp.dot(h_scr[...], w_ref[:, WP_ALR:WP_ALR + LANES], preferred_element_type=F32)
    pre = jnp.dot(a_lr.astype(BF16), wa2_ref[...], preferred_element_type=F32) + ba_ref[...]
    log_sig = jnp.minimum(pre, 0.0) - jnp.log(1.0 + jnp.exp(-jnp.abs(pre)))
    z_ref[:, Z_LA:Z_LA + LANES] = log_sig / GLA_TAU
    for lo in range(0, W_DIFF, 256):
        k_ref[:, lo:lo + 256] = jnp.dot(
            h_scr[...], w_ref[:, WP_DK + lo:WP_DK + lo + 256], preferred_element_type=F32)
        v_ref[:, lo:lo + 256] = jnp.dot(
            h_scr[...], w_ref[:, WP_DV + lo:WP_DV + lo + 256], preferred_element_type=F32)


def _inproj(x, g, w, wa2, ba):
    n = x.shape[0]
    tm = min(TOKEN_TILE, n)
    return pl.pallas_call(
        _inproj_kernel,
        grid=(n // tm,),
        in_specs=[
            pl.BlockSpec((tm, D_MODEL), lambda i: (i, 0)),
            _const_spec((1, D_MODEL)),
            _const_spec((D_MODEL, WP_W)),
            _const_spec((LANES, LANES)),
            _const_spec((1, LANES)),
        ],
        out_specs=[
            pl.BlockSpec((tm, Z_W), lambda i: (i, 0)),
            pl.BlockSpec((tm, W_DIFF), lambda i: (i, 0)),
            pl.BlockSpec((tm, W_DIFF), lambda i: (i, 0)),
        ],
        out_shape=[
            jax.ShapeDtypeStruct((n, Z_W), F32),
            jax.ShapeDtypeStruct((n, W_DIFF), F32),
            jax.ShapeDtypeStruct((n, W_DIFF), F32),
        ],
        scratch_shapes=[pltpu.VMEM((tm, D_MODEL), BF16)],
        compiler_params=_cparams(("parallel",), 48),
        name="inproj",
    )(x, g, w, wa2, ba)


def _gla_kernel(q_ref, k_ref, v_ref, la_ref, s0_ref, o_ref, st_ref, st_scr, *, n_chunks):
    c = CHUNK

    @pl.when(pl.program_id(1) == 0)
    def _():
        st_scr[...] = s0_ref[0]

    r_i = lax.broadcasted_iota(jnp.int32, (c, c), 0)
    c_i = lax.broadcasted_iota(jnp.int32, (c, c), 1)
    tri = (r_i >= c_i).astype(BF16)
    head_k = lax.broadcasted_iota(jnp.int32, (H_A * DK_A, W_GLA), 0) // DK_A
    head_v = lax.broadcasted_iota(jnp.int32, (H_A * DK_A, W_GLA), 1) // DV_A
    sum_bcast = (head_k == head_v).astype(BF16)
    st_hv = lax.broadcasted_iota(jnp.int32, (W_GLA, H_A * DK_A), 0) // DV_A
    st_hk = lax.broadcasted_iota(jnp.int32, (W_GLA, H_A * DK_A), 1) // DK_A
    st_mask = st_hv == st_hk
    sub = lax.broadcasted_iota(jnp.int32, (SUBLANES, H_A * DK_A), 0)
    n_groups = c // SUBLANES

    def chunk(ci, carry):
        rows = pl.ds(pl.multiple_of(ci * c, c), c)
        q = q_ref[0, rows, :] * (DK_A ** -0.5)
        k = k_ref[0, rows, :]
        v = v_ref[0, rows, :]
        la = la_ref[0, rows, :]

        la1 = la.astype(BF16)
        rem = la - la1.astype(F32)
        la2 = rem.astype(BF16)
        la3 = (rem - la2.astype(F32)).astype(BF16)
        b3 = jnp.dot(tri, jnp.concatenate([la1, la2, la3], axis=1), preferred_element_type=F32)
        b = b3[:, :LANES] + b3[:, LANES:2 * LANES] + b3[:, 2 * LANES:]

        pieces = []
        for s in range(c):
            g0 = s // SUBLANES
            r0 = g0 * SUBLANES
            e = jnp.exp(jnp.minimum(b[r0:, :] - b[s:s + 1, :], 0.0))
            p = q[r0:, :] * e * k[s:s + 1, :]
            if s % SUBLANES:
                first = jnp.where(sub >= (s % SUBLANES), p[:SUBLANES], 0.0)
                p = first if g0 == n_groups - 1 else jnp.concatenate([first, p[SUBLANES:]], axis=0)
            pieces.append(p)
        p_all = jnp.concatenate(pieces, axis=0).astype(BF16)
        w_all = jnp.dot(p_all, sum_bcast, preferred_element_type=F32)
        acc = [None] * n_groups
        off = 0
        for s in range(c):
            v_s = v[s:s + 1, :]
            for g in range(s // SUBLANES, n_groups):
                t = w_all[off:off + SUBLANES, :] * v_s
                acc[g] = t if acc[g] is None else acc[g] + t
                off += SUBLANES
        o_intra = jnp.concatenate(acc, axis=0)

        st = st_scr[...]
        q_dec = (q * jnp.exp(b)).astype(BF16)
        o_inter = lax.dot_general(q_dec, st.astype(BF16), (((1,), (1,)), ((), ())),
                                  preferred_element_type=F32)
        b_last = b[c - 1:c, :]
        k_dec = (k * jnp.exp(b_last - b)).astype(BF16)
        upd = lax.dot_general(v.astype(BF16), k_dec, (((0,), (0,)), ((), ())),
                              preferred_element_type=F32)
        st_scr[...] = st * jnp.exp(b_last) + jnp.where(st_mask, upd, 0.0)
        o_ref[0, rows, :] = o_intra + o_inter
        return carry

    lax.fori_loop(0, n_chunks, chunk, 0)
    st_ref[0] = st_scr[...]


def _gla(z, s0t):
    b, t, _ = z.shape
    tb = min(SCAN_TILE, t)
    kw = H_A * DK_A
    return pl.pallas_call(
        functools.partial(_gla_kernel, n_chunks=tb // CHUNK),
        grid=(b, t // tb),
        in_specs=[
            pl.BlockSpec((1, tb, kw), lambda i, j: (i, j, Z_GQ // kw)),
            pl.BlockSpec((1, tb, kw), lambda i, j: (i, j, Z_GK // kw)),
            pl.BlockSpec((1, tb, W_GLA), lambda i, j: (i, j, Z_GV // W_GLA)),
            pl.BlockSpec((1, tb, kw), lambda i, j: (i, j, Z_LA // kw)),
            pl.BlockSpec((1, W_GLA, kw), lambda i, j: (i, 0, 0)),
        ],
        out_specs=[
            pl.BlockSpec((1, tb, W_GLA), lambda i, j: (i, j, 0)),
            pl.BlockSpec((1, W_GLA, kw), lambda i, j: (i, 0, 0)),
        ],
        out_shape=[
            jax.ShapeDtypeStruct((b, t, W_GLA), F32),
            jax.ShapeDtypeStruct((b, W_GLA, kw), F32),
        ],
        scratch_shapes=[pltpu.VMEM((W_GLA, kw), F32)],
        compiler_params=_cparams(("parallel", "arbitrary"), 32),
        name="gla",
    )(z, z, z, z, s0t)


def _swap_halves(x):
    lane = lax.broadcasted_iota(jnp.int32, (x.shape[0], LANES), 1)
    first_half = (lane % DK_R) < (DK_R // 2)
    cols = []
    for lo in range(0, x.shape[1], LANES):
        blk = x[:, lo:lo + LANES]
        up = pltpu.roll(blk, LANES - DK_R // 2, axis=1)
        down = pltpu.roll(blk, DK_R // 2, axis=1)
        cols.append(jnp.where(first_half, up, down))
    return jnp.concatenate(cols, axis=1)


def _ret_kernel(q_ref, k_ref, v_ref, cos_ref, sin_ref, dec_ref, qdec_ref, kdec_ref, cdec_ref, s0_ref,
                o_ref, st_ref, st_scr, *, n_chunks, c):
    @pl.when(pl.program_id(1) == 0)
    def _():
        st_scr[...] = s0_ref[0]

    head_l = lax.broadcasted_iota(jnp.int32, (c, W_RET), 1) // DK_R
    st_mask = (lax.broadcasted_iota(jnp.int32, (W_RET, W_RET), 0) // DV_R
               == lax.broadcasted_iota(jnp.int32, (W_RET, W_RET), 1) // DK_R)

    def chunk(ci, carry):
        rows = pl.ds(pl.multiple_of(ci * c, c), c)
        cos = cos_ref[rows, :]
        sin = sin_ref[rows, :]
        q = q_ref[0, rows, :]
        k = k_ref[0, rows, :]
        v = v_ref[0, rows, :].astype(BF16)
        q = q * cos + _swap_halves(q) * sin
        k = (k * cos + _swap_halves(k) * sin) * (DK_R ** -0.5)

        q_stack = jnp.concatenate(
            [jnp.where(head_l == h, q, 0.0) for h in range(H_R)], axis=0).astype(BF16)
        s = lax.dot_general(q_stack, k.astype(BF16), (((1,), (1,)), ((), ())),
                            preferred_element_type=F32)
        s = (s * dec_ref[...]).astype(BF16)
        r = jnp.dot(s, v, preferred_element_type=F32)
        o = jnp.where(head_l == 0, r[:c], 0.0)
        for h in range(1, H_R):
            o = o + jnp.where(head_l == h, r[h * c:(h + 1) * c], 0.0)

        st = st_scr[...]
        o = o + lax.dot_general((q * qdec_ref[...]).astype(BF16), st.astype(BF16),
                                (((1,), (1,)), ((), ())), preferred_element_type=F32)
        upd = lax.dot_general(v, (k * kdec_ref[...]).astype(BF16), (((0,), (0,)), ((), ())),
                              preferred_element_type=F32)
        st_scr[...] = st * cdec_ref[...] + jnp.where(st_mask, upd, 0.0)
        o_ref[0, rows, :] = o
        return carry

    lax.fori_loop(0, n_chunks, chunk, 0)
    st_ref[0] = st_scr[...]


def _ret_tables(c):
    lg = jnp.log(1.0 - 2.0 ** (-5.0 - jnp.arange(H_R, dtype=F32)))
    idx = jnp.arange(c, dtype=F32)
    tri = idx[:, None] >= idx[None, :]
    dec = jnp.exp(jnp.where(tri[None], (idx[:, None] - idx[None, :])[None] * lg[:, None, None], -jnp.inf))
    q_dec = jnp.exp((idx[None, :] + 1.0) * lg[:, None])
    k_dec = jnp.exp((c - 1.0 - idx[None, :]) * lg[:, None])
    c_dec = jnp.exp(c * lg)
    return (dec.reshape(H_R * c, c),
            jnp.repeat(q_dec.T, DK_R, axis=1),
            jnp.repeat(k_dec.T, DK_R, axis=1),
            jnp.repeat(c_dec, DK_R)[None, :])


def _rope_tables(pos):
    half = DK_R // 2
    inv = ROPE_BASE ** (-jnp.arange(half, dtype=F32) / half)
    ang = pos.astype(F32)[:, None] * inv[None, :]
    cos = jnp.cos(ang)
    sin = jnp.sin(ang)
    cos_t = jnp.tile(jnp.concatenate([cos, cos], axis=1), (1, H_R))
    sin_t = jnp.tile(jnp.concatenate([-sin, sin], axis=1), (1, H_R))
    return cos_t, sin_t


def _ret(z, s0t, past_len):
    b, t, _ = z.shape
    tb = min(SCAN_TILE, t)
    c = min(RET_CHUNK, t)
    cos_t, sin_t = _rope_tables(past_len + jnp.arange(t))
    dec, q_dec, k_dec, c_dec = _ret_tables(c)
    return pl.pallas_call(
        functools.partial(_ret_kernel, n_chunks=tb // c, c=c),
        grid=(b, t // tb),
        in_specs=[
            pl.BlockSpec((1, tb, W_RET), lambda i, j: (i, j, Z_RQ // W_RET)),
            pl.BlockSpec((1, tb, W_RET), lambda i, j: (i, j, Z_RK // W_RET)),
            pl.BlockSpec((1, tb, W_RET), lambda i, j: (i, j, Z_RV // W_RET)),
            pl.BlockSpec((tb, W_RET), lambda i, j: (j, 0)),
            pl.BlockSpec((tb, W_RET), lambda i, j: (j, 0)),
            _const_spec((H_R * c, c)),
            _const_spec((c, W_RET)),
            _const_spec((c, W_RET)),
            _const_spec((1, W_RET)),
            pl.BlockSpec((1, W_RET, W_RET), lambda i, j: (i, 0, 0)),
        ],
        out_specs=[
            pl.BlockSpec((1, tb, W_RET), lambda i, j: (i, j, 0)),
            pl.BlockSpec((1, W_RET, W_RET), lambda i, j: (i, 0, 0)),
        ],
        out_shape=[
            jax.ShapeDtypeStruct((b, t, W_RET), F32),
            jax.ShapeDtypeStruct((b, W_RET, W_RET), F32),
        ],
        scratch_shapes=[pltpu.VMEM((W_RET, W_RET), F32)],
        compiler_params=_cparams(("parallel", "arbitrary"), 32),
        name="ret",
    )(z, z, z, cos_t, sin_t, dec, q_dec, k_dec, c_dec, s0t)


def _attn_kernel(lam_ref, slope_ref, q_ref, kc_ref, vc_ref, kd_ref, vd_ref, g_ref, o_ref,
                 m_scr, l_scr, acc_scr, *, tq, tk, n_ctx_static, q_off, out_scale):
    h = pl.program_id(1)
    qi = pl.program_id(2)
    slope = slope_ref[h]
    lam = lam_ref[0]

    q = q_ref[0] * (DK_D ** -0.5)
    lane = lax.broadcasted_iota(jnp.int32, (tq, 2 * DK_D), 1)
    q_stack = jnp.concatenate(
        [jnp.where(lane < DK_D, q, 0.0), jnp.where(lane >= DK_D, q, 0.0)], axis=0).astype(BF16)

    m_scr[...] = jnp.full(m_scr.shape, NEG_INF, F32)
    l_scr[...] = jnp.zeros(l_scr.shape, F32)
    acc_scr[...] = jnp.zeros(acc_scr.shape, F32)

    def online_update(s, v_tile, shift):
        m_old = m_scr[...]
        m_new = jnp.maximum(m_old, jnp.max(s, axis=-1, keepdims=True) + shift)
        p = jnp.exp(s - (m_new - shift))
        alpha = jnp.exp(m_old - m_new)
        l_scr[...] = alpha * l_scr[...] + jnp.sum(p, axis=-1, keepdims=True)
        acc_scr[...] = alpha * acc_scr[...] + jnp.dot(p.astype(BF16), v_tile, preferred_element_type=F32)
        m_scr[...] = m_new

    q0 = q_off + qi * tq
    r_c = lax.broadcasted_iota(jnp.int32, (tq, tk), 0)
    c_c = lax.broadcasted_iota(jnp.int32, (tq, tk), 1)
    rel = (c_c - r_c).astype(F32) * slope
    rel2 = jnp.concatenate([rel, rel], axis=0)

    def ctx_tile(j, carry):
        rows = pl.ds(pl.multiple_of(j * tk, tk), tk)
        k_tile = kc_ref[0, rows, :].astype(BF16)
        v_tile = vc_ref[0, rows, :].astype(BF16)
        s = lax.dot_general(q_stack, k_tile, (((1,), (1,)), ((), ())), preferred_element_type=F32) + rel2
        shift = -slope * (q0 - j * tk).astype(F32)
        online_update(s, v_tile, shift)
        return carry

    n_ctx = qi * (tq // tk) if n_ctx_static is None else n_ctx_static
    lax.fori_loop(0, n_ctx, ctx_tile, 0)

    r_d = lax.broadcasted_iota(jnp.int32, (tq, tq), 0)
    c_d = lax.broadcasted_iota(jnp.int32, (tq, tq), 1)
    bias = -slope * jnp.abs(r_d - c_d).astype(F32)
    visible = (c_d // CHUNK) <= (r_d // CHUNK)
    bias = jnp.where(visible, bias, NEG_INF)
    s = lax.dot_general(q_stack, kd_ref[0].astype(BF16), (((1,), (1,)), ((), ())),
                        preferred_element_type=F32)
    vis2 = jnp.concatenate([visible, visible], axis=0)
    s = jnp.where(vis2, s + jnp.concatenate([bias, bias], axis=0), NEG_INF)
    online_update(s, vd_ref[0].astype(BF16), 0.0)

    o_all = acc_scr[...] / l_scr[...]
    o = o_all[:tq] - lam * o_all[tq:]
    o_ref[0] = _rms(o, g_ref[...]) * out_scale


def _attn(lam, q_src, q_lane_blk, k_ctx, v_ctx, k_new, v_new, g, *, t_q, tq, tk, n_ctx_static, q_off,
          out_scale):
    b = q_src.shape[0]
    t_ctx = k_ctx.shape[1]
    slopes = jnp.asarray([2.0 ** (-8.0 * (i + 1) / H_D) for i in range(H_D)], F32)
    hw = 2 * DK_D
    kern = functools.partial(_attn_kernel, tq=tq, tk=tk, n_ctx_static=n_ctx_static, q_off=q_off,
                             out_scale=out_scale)
    smem = pl.BlockSpec(memory_space=pltpu.SMEM)
    return pl.pallas_call(
        kern,
        grid=(b, H_D, t_q // tq),
        in_specs=[
            smem, smem,
            pl.BlockSpec((1, tq, hw), lambda i, h, j: (i, j, q_lane_blk + h)),
            pl.BlockSpec((1, t_ctx, hw), lambda i, h, j: (i, 0, h)),
            pl.BlockSpec((1, t_ctx, DV_D), lambda i, h, j: (i, 0, h)),
            pl.BlockSpec((1, tq, hw), lambda i, h, j: (i, j, h)),
            pl.BlockSpec((1, tq, DV_D), lambda i, h, j: (i, j, h)),
            pl.BlockSpec((1, DV_D), lambda i, h, j: (0, 0)),
        ],
        out_specs=pl.BlockSpec((1, tq, DV_D), lambda i, h, j: (i, j, h)),
        out_shape=jax.ShapeDtypeStruct((b, t_q, W_DIFF), F32),
        scratch_shapes=[
            pltpu.VMEM((2 * tq, 1), F32),
            pltpu.VMEM((2 * tq, 1), F32),
            pltpu.VMEM((2 * tq, DV_D), F32),
        ],
        compiler_params=_cparams(("parallel", "parallel", "arbitrary"), 32),
        name="attn",
    )(lam, slopes, q_src, k_ctx, v_ctx, k_new, v_new, g)


def _seg_mean(x, avg):
    hi = x.astype(BF16)
    lo = (x - hi.astype(F32)).astype(BF16)
    return jnp.dot(jnp.concatenate([hi, lo], axis=1), avg, preferred_element_type=F32)


def _outproj_kernel(x_ref, oa_ref, gr_ref, od_ref, or_ref, rg_ref, ga_ref, gret_ref, w_ref, gpost_ref,
                    o_ref, cat_scr):
    lane_r = lax.broadcasted_iota(jnp.int32, (2 * W_GLA, W_GLA), 0) % W_GLA // DV_A
    lane_c = lax.broadcasted_iota(jnp.int32, (2 * W_GLA, W_GLA), 1) // DV_A
    avg = jnp.where(lane_r == lane_c, 1.0 / DV_A, 0.0).astype(BF16)

    oa = oa_ref[...]
    gr = gr_ref[...]
    oa_n = oa * lax.rsqrt(_seg_mean(oa * oa, avg) + EPS) * ga_ref[...]
    cat_scr[:, 0:W_GLA] = (oa_n * (gr * _sigmoid(gr))).astype(BF16)

    cat_scr[:, W_GLA:W_GLA + W_DIFF] = od_ref[...].astype(BF16)

    orr = or_ref[...]
    rg = rg_ref[...]
    cen = orr - _seg_mean(orr, avg)
    or_n = cen * lax.rsqrt(_seg_mean(cen * cen, avg) + EPS) * gret_ref[...]
    cat_scr[:, W_GLA + W_DIFF:] = (or_n * (rg * _sigmoid(rg))).astype(BF16)

    m = jnp.dot(cat_scr[...], w_ref[...], preferred_element_type=F32)
    o_ref[...] = x_ref[...] + _rms(m, gpost_ref[...])


def _outproj(x, oa, z, od, orr, ga, gret, w, gpost):
    n = x.shape[0]
    tm = min(TOKEN_TILE, n)
    return pl.pallas_call(
        _outproj_kernel,
        grid=(n // tm,),
        in_specs=[
            pl.BlockSpec((tm, D_MODEL), lambda i: (i, 0)),
            pl.BlockSpec((tm, W_GLA), lambda i: (i, 0)),
            pl.BlockSpec((tm, W_GLA), lambda i: (i, Z_GR // W_GLA)),
            pl.BlockSpec((tm, W_DIFF), lambda i: (i, 0)),
            pl.BlockSpec((tm, W_RET), lambda i: (i, 0)),
            pl.BlockSpec((tm, W_RET), lambda i: (i, Z_RG // W_RET)),
            _const_spec((1, W_GLA)),
            _const_spec((1, W_RET)),
            _const_spec((D_MODEL, D_MODEL)),
            _const_spec((1, D_MODEL)),
        ],
        out_specs=pl.BlockSpec((tm, D_MODEL), lambda i: (i, 0)),
        out_shape=jax.ShapeDtypeStruct((n, D_MODEL), F32),
        scratch_shapes=[pltpu.VMEM((tm, D_MODEL), BF16)],
        compiler_params=_cparams(("parallel",), 32),
        name="outproj",
    )(x, oa, z, od, orr, z, ga, gret, w, gpost)


def _regroup_w_in(w):
    offs = [0]
    for s in (GLA_RANK, H_A * DK_A, H_A * DK_A, W_GLA, W_GLA, 2 * H_D * DK_D, 2 * H_D * DK_D, W_DIFF,
              H_R * DK_R, H_R * DK_R, W_RET, W_RET):
        offs.append(offs[-1] + s)
    a_lr, gq, gk, gv, gr, dq, dk, dv, rq, rk, rv, rg = (w[:, offs[i]:offs[i + 1]] for i in range(12))
    a_pad = jnp.pad(a_lr, ((0, 0), (0, LANES - GLA_RANK)))
    return jnp.concatenate([gv, gr, rq, rk, rv, rg, dq, gq, gk, a_pad, dk, dv], axis=1).astype(BF16)


def _state_to_blockdiag_t(s):
    b, h, dk, dv = s.shape
    eye = jnp.eye(h, dtype=s.dtype)
    return jnp.einsum('bhdv,hg->bhvgd', s, eye).reshape(b, h * dv, h * dk)


def _blockdiag_t_to_state(st, h, dk, dv):
    b = st.shape[0]
    blocks = st.reshape(b, h, dv, h, dk)
    diag = jnp.stack([blocks[:, i, :, i, :] for i in range(h)], axis=1)
    return diag.swapaxes(-1, -2)


def _layer(x, l, p, past_k, past_v, s_gla0, s_ret0):
    b, t, _ = x.shape
    n = b * t
    x = _ffn(x.reshape(n, D_MODEL), p['ffn1_norm_pre'][l], p['ffn1_w_up'][l], p['ffn1_w_down'][l],
             p['ffn1_norm_post'][l])
    z, k_rows, v_rows = _inproj(x, p['mix_norm_pre'][l], p['w_in'][l], p['w_gla_a2'][l], p['b_gla_a'][l])
    z3 = z.reshape(b, t, Z_W)
    k3 = k_rows.reshape(b, t, W_DIFF)
    v3 = v_rows.reshape(b, t, W_DIFF)

    o_a, st_gla = _gla(z3, s_gla0)

    lam_init = 0.8 - 0.6 * math.exp(-0.3 * l)
    attn_kw = dict(t_q=t, q_off=0 if past_k is None else past_k.shape[1], out_scale=1.0 - lam_init)
    q_blk = Z_DQ // (2 * DK_D)
    if past_k is None:
        tq = min(ATTN_TILE, t)
        o_d = _attn(p['lam'][l], z3, q_blk, k3, v3, k3, v3, p['diff_subln'][l],
                    tq=tq, tk=tq, n_ctx_static=None, **attn_kw)
    else:
        tk = min(ATTN_CTX_TILE, past_k.shape[1])
        o_d = _attn(p['lam'][l], z3, q_blk, past_k, past_v, k3, v3, p['diff_subln'][l],
                    tq=t, tk=tk, n_ctx_static=past_k.shape[1] // tk, **attn_kw)

    o_r, st_ret = _ret(z3, s_ret0, attn_kw['q_off'])

    x = _outproj(x, o_a.reshape(n, W_GLA), z, o_d.reshape(n, W_DIFF), o_r.reshape(n, W_RET),
                 p['gla_norm'][l], p['ret_norm'][l], p['w_out'][l], p['mix_norm_post'][l])
    x = _ffn(x, p['ffn2_norm_pre'][l], p['ffn2_w_up'][l], p['ffn2_w_down'][l], p['ffn2_norm_post'][l])
    return x.reshape(b, t, D_MODEL), k3, v3, st_gla, st_ret


def kernel(x_prompt, x_sample, cache_diff_k, cache_diff_v, state_gla, state_ret, ffn1_norm_pre, ffn1_w_up, ffn1_w_down, ffn1_norm_post, mix_norm_pre, w_in, w_gla_a2, b_gla_a, gla_norm, diff_lambda, diff_subln, ret_norm, w_out, mix_norm_post, ffn2_norm_pre, ffn2_w_up, ffn2_w_down, ffn2_norm_post):
    depth = w_in.shape[0]
    row = lambda a: a.reshape(depth, 1, -1)
    lam_p = diff_lambda.astype(F32)
    lam_init = jnp.asarray([0.8 - 0.6 * math.exp(-0.3 * l) for l in range(depth)], F32)
    lam = (jnp.exp(jnp.sum(lam_p[:, 0] * lam_p[:, 1], axis=-1))
           - jnp.exp(jnp.sum(lam_p[:, 2] * lam_p[:, 3], axis=-1)) + lam_init)
    p = {
        'ffn1_norm_pre': row(ffn1_norm_pre), 'ffn1_w_up': ffn1_w_up.astype(BF16),
        'ffn1_w_down': ffn1_w_down.astype(BF16), 'ffn1_norm_post': row(ffn1_norm_post),
        'mix_norm_pre': row(mix_norm_pre),
        'w_in': jnp.stack([_regroup_w_in(w_in[l]) for l in range(depth)]),
        'w_gla_a2': jnp.pad(w_gla_a2, ((0, 0), (0, LANES - GLA_RANK), (0, 0))).astype(BF16),
        'b_gla_a': row(b_gla_a), 'gla_norm': row(gla_norm), 'lam': lam.reshape(depth, 1),
        'diff_subln': row(diff_subln), 'ret_norm': row(ret_norm), 'w_out': w_out.astype(BF16),
        'mix_norm_post': row(mix_norm_post),
        'ffn2_norm_pre': row(ffn2_norm_pre), 'ffn2_w_up': ffn2_w_up.astype(BF16),
        'ffn2_w_down': ffn2_w_down.astype(BF16), 'ffn2_norm_post': row(ffn2_norm_post),
    }
    bp = x_prompt.shape[0]
    bs = x_sample.shape[0]
    zeros_gla = jnp.zeros((bp, W_GLA, H_A * DK_A), F32)
    zeros_ret = jnp.zeros((bp, W_RET, H_R * DK_R), F32)
    xp, xs = x_prompt, x_sample
    outs_p, outs_s = [], []
    for l in range(depth):
        xp, *rest = _layer(xp, l, p, None, None, zeros_gla, zeros_ret)
        outs_p.append(rest)
        past_k = cache_diff_k[l].reshape(bs, -1, W_DIFF)
        past_v = cache_diff_v[l].reshape(bs, -1, W_DIFF)
        xs, *rest = _layer(xs, l, p, past_k, past_v, _state_to_blockdiag_t(state_gla[l]),
                           _state_to_blockdiag_t(state_ret[l]))
        outs_s.append(rest)

    def collect(outs, b):
        k = jnp.stack([o[0] for o in outs]).reshape(depth, b, -1, H_D, 2 * DK_D)
        v = jnp.stack([o[1] for o in outs]).reshape(depth, b, -1, H_D, DV_D)
        g = jnp.stack([_blockdiag_t_to_state(o[2], H_A, DK_A, DV_A) for o in outs])
        r = jnp.stack([_blockdiag_t_to_state(o[3], H_R, DK_R, DV_R) for o in outs])
        return k, v, g, r

    kp, vp, gp, rp = collect(outs_p, bp)
    ks, vs, gs, rs = collect(outs_s, bs)
    return (xp, xs, kp, vp, gp, rp, ks, vs, gs, rs)
```

```python
import functools
import math

import jax
import jax.numpy as jnp
from jax import lax
from jax.experimental import pallas as pl
from jax.experimental.pallas import tpu as pltpu

F32 = jnp.float32
BF16 = jnp.bfloat16

D_MODEL = 1024
D_FF = 2816
EPS = 1e-6
NEG_INF = -1e30
CHUNK = 64

H_A, DK_A, DV_A = 4, 32, 64
GLA_RANK = 16
GLA_TAU = 16.0
H_D, DK_D, DV_D = 4, 64, 128
H_R, DK_R, DV_R = 4, 64, 64
ROPE_BASE = 10000.0

W_GLA = H_A * DV_A
W_DIFF = H_D * DV_D
W_RET = H_R * DV_R

LANES = 128
SUBLANES = 8

Z_GV, Z_GR, Z_RQ, Z_RK, Z_RV, Z_RG = 0, 256, 512, 768, 1024, 1280
Z_DQ, Z_GQ, Z_GK, Z_LA = 1536, 2048, 2176, 2304
Z_W = 2432
WP_ALR = 2304
WP_DK = 2432
WP_DV = 2944
WP_W = 3456

TOKEN_TILE = 512
FF_COLS = 256
SCAN_TILE = 512
RET_CHUNK = 128
ATTN_TILE = 256
ATTN_CTX_TILE = 512

_MIB = 1024 * 1024


def _cparams(sem, vmem_mib):
    return pltpu.CompilerParams(dimension_semantics=sem, vmem_limit_bytes=vmem_mib * _MIB)


def _rms(x, g):
    return x * lax.rsqrt(jnp.mean(x * x, axis=-1, keepdims=True) + EPS) * g


def _sigmoid(x):
    return 1.0 / (1.0 + jnp.exp(-x))


def _const_spec(shape):
    n = len(shape)
    return pl.BlockSpec(shape, lambda *_: (0,) * n, pipeline_mode=pl.Buffered(1))


def _ffn_kernel(x_ref, gpre_ref, wup_ref, wdown_ref, gpost_ref, o_ref, h_scr, act_scr):
    x = x_ref[...]
    h_scr[...] = _rms(x, gpre_ref[...]).astype(BF16)
    for c in range(D_FF // FF_COLS):
        lo = c * FF_COLS
        gate = jnp.dot(h_scr[...], wup_ref[:, lo:lo + FF_COLS], preferred_element_type=F32)
        up = jnp.dot(h_scr[...], wup_ref[:, D_FF + lo:D_FF + lo + FF_COLS], preferred_element_type=F32)
        act_scr[:, lo:lo + FF_COLS] = (gate * _sigmoid(gate) * up).astype(BF16)
    y = jnp.dot(act_scr[...], wdown_ref[...], preferred_element_type=F32)
    o_ref[...] = x + 0.5 * _rms(y, gpost_ref[...])


def _ffn(x, gpre, wup, wdown, gpost):
    n = x.shape[0]
    tm = min(TOKEN_TILE, n)
    return pl.pallas_call(
        _ffn_kernel,
        grid=(n // tm,),
        in_specs=[
            pl.BlockSpec((tm, D_MODEL), lambda i: (i, 0)),
            _const_spec((1, D_MODEL)),
            _const_spec((D_MODEL, 2 * D_FF)),
            _const_spec((D_FF, D_MODEL)),
            _const_spec((1, D_MODEL)),
        ],
        out_specs=pl.BlockSpec((tm, D_MODEL), lambda i: (i, 0)),
        out_shape=jax.ShapeDtypeStruct((n, D_MODEL), F32),
        scratch_shapes=[pltpu.VMEM((tm, D_MODEL), BF16), pltpu.VMEM((tm, D_FF), BF16)],
        compiler_params=_cparams(("parallel",), 48),
        name="ffn",
    )(x, gpre, wup, wdown, gpost)


def _inproj_kernel(*refs, n_alias):
    x_ref, g_ref, w_ref, wa2_ref, ba_ref = refs[:5]
    z_ref, k4_ref, v4_ref, kb_ref, vb_ref, h_scr = refs[5 + n_alias:]
    h_scr[...] = _rms(x_ref[...], g_ref[...]).astype(BF16)
    for lo in range(0, WP_ALR, 256):
        z_ref[:, lo:lo + 256] = jnp.dot(h_scr[...], w_ref[:, lo:lo + 256], preferred_element_type=F32)
    a_lr = jnp.dot(h_scr[...], w_ref[:, WP_ALR:WP_ALR + LANES], preferred_element_type=F32)
    pre = jnp.dot(a_lr.astype(BF16), wa2_ref[...], preferred_element_type=F32) + ba_ref[...]
    log_sig = jnp.minimum(pre, 0.0) - jnp.log(1.0 + jnp.exp(-jnp.abs(pre)))
    z_ref[:, Z_LA:Z_LA + LANES] = log_sig / GLA_TAU
    hw = 2 * DK_D
    for h in range(H_D):
        k_h = jnp.dot(h_scr[...], w_ref[:, WP_DK + h * hw:WP_DK + (h + 1) * hw], preferred_element_type=F32)
        v_h = jnp.dot(h_scr[...], w_ref[:, WP_DV + h * hw:WP_DV + (h + 1) * hw], preferred_element_type=F32)
        k4_ref[0, :, h, :] = k_h
        v4_ref[0, :, h, :] = v_h
        kb_ref[:, h * hw:(h + 1) * hw] = k_h.astype(BF16)
        vb_ref[:, h * hw:(h + 1) * hw] = v_h.astype(BF16)


def _inproj(x, g, w, wa2, ba, layer, depth, kv_prev):
    n = x.shape[0]
    tm = min(TOKEN_TILE, n)
    hw = 2 * DK_D
    n_alias = 0 if kv_prev is None else 2
    kv_shape = jax.ShapeDtypeStruct((depth, n, H_D, hw), F32)
    kv_spec = pl.BlockSpec((1, tm, H_D, hw), lambda i: (layer, i, 0, 0))
    return pl.pallas_call(
        functools.partial(_inproj_kernel, n_alias=n_alias),
        grid=(n // tm,),
        in_specs=[
            pl.BlockSpec((tm, D_MODEL), lambda i: (i, 0)),
            _const_spec((1, D_MODEL)),
            _const_spec((D_MODEL, WP_W)),
            _const_spec((LANES, LANES)),
            _const_spec((1, LANES)),
        ] + [pl.BlockSpec(memory_space=pl.ANY)] * n_alias,
        out_specs=[
            pl.BlockSpec((tm, Z_W), lambda i: (i, 0)),
            kv_spec, kv_spec,
            pl.BlockSpec((tm, W_DIFF), lambda i: (i, 0)),
            pl.BlockSpec((tm, W_DIFF), lambda i: (i, 0)),
        ],
        out_shape=[
            jax.ShapeDtypeStruct((n, Z_W), F32),
            kv_shape, kv_shape,
            jax.ShapeDtypeStruct((n, W_DIFF), BF16),
            jax.ShapeDtypeStruct((n, W_DIFF), BF16),
        ],
        input_output_aliases={} if kv_prev is None else {5: 1, 6: 2},
        scratch_shapes=[pltpu.VMEM((tm, D_MODEL), BF16)],
        compiler_params=_cparams(("parallel",), 48),
        name="inproj",
    )(x, g, w, wa2, ba, *(() if kv_prev is None else kv_prev))


def _gla_kernel(q_ref, k_ref, v_ref, la_ref, s0_ref, o_ref, st_ref, st_scr, *, n_chunks):
    c = CHUNK

    @pl.when(pl.program_id(1) == 0)
    def _():
        st_scr[...] = s0_ref[0]

    r_i = lax.broadcasted_iota(jnp.int32, (c, c), 0)
    c_i = lax.broadcasted_iota(jnp.int32, (c, c), 1)
    tri = (r_i >= c_i).astype(BF16)
    head_k = lax.broadcasted_iota(jnp.int32, (H_A * DK_A, W_GLA), 0) // DK_A
    head_v = lax.broadcasted_iota(jnp.int32, (H_A * DK_A, W_GLA), 1) // DV_A
    sum_bcast = (head_k == head_v).astype(BF16)
    st_hv = lax.broadcasted_iota(jnp.int32, (W_GLA, H_A * DK_A), 0) // DV_A
    st_hk = lax.broadcasted_iota(jnp.int32, (W_GLA, H_A * DK_A), 1) // DK_A
    st_mask = st_hv == st_hk
    sub = lax.broadcasted_iota(jnp.int32, (SUBLANES, H_A * DK_A), 0)
    n_groups = c // SUBLANES

    def chunk(ci, carry):
        rows = pl.ds(pl.multiple_of(ci * c, c), c)
        q = q_ref[0, rows, :] * (DK_A ** -0.5)
        k = k_ref[0, rows, :]
        v = v_ref[0, rows, :]
        la = la_ref[0, rows, :]

        la1 = la.astype(BF16)
        rem = la - la1.astype(F32)
        la2 = rem.astype(BF16)
        la3 = (rem - la2.astype(F32)).astype(BF16)
        b3 = jnp.dot(tri, jnp.concatenate([la1, la2, la3], axis=1), preferred_element_type=F32)
        b = b3[:, :LANES] + b3[:, LANES:2 * LANES] + b3[:, 2 * LANES:]

        pieces = []
        for s in range(c):
            g0 = s // SUBLANES
            r0 = g0 * SUBLANES
            e = jnp.exp(jnp.minimum(b[r0:, :] - b[s:s + 1, :], 0.0))
            p = q[r0:, :] * e * k[s:s + 1, :]
            if s % SUBLANES:
                first = jnp.where(sub >= (s % SUBLANES), p[:SUBLANES], 0.0)
                p = first if g0 == n_groups - 1 else jnp.concatenate([first, p[SUBLANES:]], axis=0)
            pieces.append(p)
        p_all = jnp.concatenate(pieces, axis=0).astype(BF16)
        w_all = jnp.dot(p_all, sum_bcast, preferred_element_type=F32)
        acc = [None] * n_groups
        off = 0
        for s in range(c):
            v_s = v[s:s + 1, :]
            for g in range(s // SUBLANES, n_groups):
                t = w_all[off:off + SUBLANES, :] * v_s
                acc[g] = t if acc[g] is None else acc[g] + t
                off += SUBLANES
        o_intra = jnp.concatenate(acc, axis=0)

        st = st_scr[...]
        q_dec = (q * jnp.exp(b)).astype(BF16)
        o_inter = lax.dot_general(q_dec, st.astype(BF16), (((1,), (1,)), ((), ())),
                                  preferred_element_type=F32)
        b_last = b[c - 1:c, :]
        k_dec = (k * jnp.exp(b_last - b)).astype(BF16)
        upd = lax.dot_general(v.astype(BF16), k_dec, (((0,), (0,)), ((), ())),
                              preferred_element_type=F32)
        st_scr[...] = st * jnp.exp(b_last) + jnp.where(st_mask, upd, 0.0)
        o_ref[0, rows, :] = o_intra + o_inter
        return carry

    lax.fori_loop(0, n_chunks, chunk, 0)
    st_ref[0] = st_scr[...]


def _gla(z, s0t):
    b, t, _ = z.shape
    tb = min(SCAN_TILE, t)
    kw = H_A * DK_A
    return pl.pallas_call(
        functools.partial(_gla_kernel, n_chunks=tb // CHUNK),
        grid=(b, t // tb),
        in_specs=[
            pl.BlockSpec((1, tb, kw), lambda i, j: (i, j, Z_GQ // kw)),
            pl.BlockSpec((1, tb, kw), lambda i, j: (i, j, Z_GK // kw)),
            pl.BlockSpec((1, tb, W_GLA), lambda i, j: (i, j, Z_GV // W_GLA)),
            pl.BlockSpec((1, tb, kw), lambda i, j: (i, j, Z_LA // kw)),
            pl.BlockSpec((1, W_GLA, kw), lambda i, j: (i, 0, 0)),
        ],
        out_specs=[
            pl.BlockSpec((1, tb, W_GLA), lambda i, j: (i, j, 0)),
            pl.BlockSpec((1, W_GLA, kw), lambda i, j: (i, 0, 0)),
        ],
        out_shape=[
            jax.ShapeDtypeStruct((b, t, W_GLA), F32),
            jax.ShapeDtypeStruct((b, W_GLA, kw), F32),
        ],
        scratch_shapes=[pltpu.VMEM((W_GLA, kw), F32)],
        compiler_params=_cparams(("parallel", "arbitrary"), 32),
        name="gla",
    )(z, z, z, z, s0t)


def _swap_halves(x):
    lane = lax.broadcasted_iota(jnp.int32, (x.shape[0], LANES), 1)
    first_half = (lane % DK_R) < (DK_R // 2)
    cols = []
    for lo in range(0, x.shape[1], LANES):
        blk = x[:, lo:lo + LANES]
        up = pltpu.roll(blk, LANES - DK_R // 2, axis=1)
        down = pltpu.roll(blk, DK_R // 2, axis=1)
        cols.append(jnp.where(first_half, up, down))
    return jnp.concatenate(cols, axis=1)


def _ret_kernel(q_ref, k_ref, v_ref, cos_ref, sin_ref, dec_ref, qdec_ref, kdec_ref, cdec_ref, s0_ref,
                o_ref, st_ref, st_scr, *, n_chunks, c):
    @pl.when(pl.program_id(1) == 0)
    def _():
        st_scr[...] = s0_ref[0]

    head_l = lax.broadcasted_iota(jnp.int32, (c, W_RET), 1) // DK_R
    st_mask = (lax.broadcasted_iota(jnp.int32, (W_RET, W_RET), 0) // DV_R
               == lax.broadcasted_iota(jnp.int32, (W_RET, W_RET), 1) // DK_R)

    def chunk(ci, carry):
        rows = pl.ds(pl.multiple_of(ci * c, c), c)
        cos = cos_ref[rows, :]
        sin = sin_ref[rows, :]
        q = q_ref[0, rows, :]
        k = k_ref[0, rows, :]
        v = v_ref[0, rows, :].astype(BF16)
        q = q * cos + _swap_halves(q) * sin
        k = (k * cos + _swap_halves(k) * sin) * (DK_R ** -0.5)

        q_stack = jnp.concatenate(
            [jnp.where(head_l == h, q, 0.0) for h in range(H_R)], axis=0).astype(BF16)
        s = lax.dot_general(q_stack, k.astype(BF16), (((1,), (1,)), ((), ())),
                            preferred_element_type=F32)
        s = (s * dec_ref[...]).astype(BF16)
        r = jnp.dot(s, v, preferred_element_type=F32)
        o = jnp.where(head_l == 0, r[:c], 0.0)
        for h in range(1, H_R):
            o = o + jnp.where(head_l == h, r[h * c:(h + 1) * c], 0.0)

        st = st_scr[...]
        o = o + lax.dot_general((q * qdec_ref[...]).astype(BF16), st.astype(BF16),
                                (((1,), (1,)), ((), ())), preferred_element_type=F32)
        upd = lax.dot_general(v, (k * kdec_ref[...]).astype(BF16), (((0,), (0,)), ((), ())),
                              preferred_element_type=F32)
        st_scr[...] = st * cdec_ref[...] + jnp.where(st_mask, upd, 0.0)
        o_ref[0, rows, :] = o
        return carry

    lax.fori_loop(0, n_chunks, chunk, 0)
    st_ref[0] = st_scr[...]


def _ret_tables(c):
    lg = jnp.log(1.0 - 2.0 ** (-5.0 - jnp.arange(H_R, dtype=F32)))
    idx = jnp.arange(c, dtype=F32)
    tri = idx[:, None] >= idx[None, :]
    dec = jnp.exp(jnp.where(tri[None], (idx[:, None] - idx[None, :])[None] * lg[:, None, None], -jnp.inf))
    q_dec = jnp.exp((idx[None, :] + 1.0) * lg[:, None])
    k_dec = jnp.exp((c - 1.0 - idx[None, :]) * lg[:, None])
    c_dec = jnp.exp(c * lg)
    return (dec.reshape(H_R * c, c),
            jnp.repeat(q_dec.T, DK_R, axis=1),
            jnp.repeat(k_dec.T, DK_R, axis=1),
            jnp.repeat(c_dec, DK_R)[None, :])


def _rope_tables(pos):
    half = DK_R // 2
    inv = ROPE_BASE ** (-jnp.arange(half, dtype=F32) / half)
    ang = pos.astype(F32)[:, None] * inv[None, :]
    cos = jnp.cos(ang)
    sin = jnp.sin(ang)
    cos_t = jnp.tile(jnp.concatenate([cos, cos], axis=1), (1, H_R))
    sin_t = jnp.tile(jnp.concatenate([-sin, sin], axis=1), (1, H_R))
    return cos_t, sin_t


def _ret(z, s0t, past_len):
    b, t, _ = z.shape
    tb = min(SCAN_TILE, t)
    c = min(RET_CHUNK, t)
    cos_t, sin_t = _rope_tables(past_len + jnp.arange(t))
    dec, q_dec, k_dec, c_dec = _ret_tables(c)
    return pl.pallas_call(
        functools.partial(_ret_kernel, n_chunks=tb // c, c=c),
        grid=(b, t // tb),
        in_specs=[
            pl.BlockSpec((1, tb, W_RET), lambda i, j: (i, j, Z_RQ // W_RET)),
            pl.BlockSpec((1, tb, W_RET), lambda i, j: (i, j, Z_RK // W_RET)),
            pl.BlockSpec((1, tb, W_RET), lambda i, j: (i, j, Z_RV // W_RET)),
            pl.BlockSpec((tb, W_RET), lambda i, j: (j, 0)),
            pl.BlockSpec((tb, W_RET), lambda i, j: (j, 0)),
            _const_spec((H_R * c, c)),
            _const_spec((c, W_RET)),
            _const_spec((c, W_RET)),
            _const_spec((1, W_RET)),
            pl.BlockSpec((1, W_RET, W_RET), lambda i, j: (i, 0, 0)),
        ],
        out_specs=[
            pl.BlockSpec((1, tb, W_RET), lambda i, j: (i, j, 0)),
            pl.BlockSpec((1, W_RET, W_RET), lambda i, j: (i, 0, 0)),
        ],
        out_shape=[
            jax.ShapeDtypeStruct((b, t, W_RET), F32),
            jax.ShapeDtypeStruct((b, W_RET, W_RET), F32),
        ],
        scratch_shapes=[pltpu.VMEM((W_RET, W_RET), F32)],
        compiler_params=_cparams(("parallel", "arbitrary"), 32),
        name="ret",
    )(z, z, z, cos_t, sin_t, dec, q_dec, k_dec, c_dec, s0t)


V_ROWS = DV_D + 16


LOG2E = 1.4426950408889634


def _stack_maps(q):
    lane = lax.broadcasted_iota(jnp.int32, q.shape, 1)
    q = q * (DK_D ** -0.5 * LOG2E)
    return jnp.concatenate(
        [jnp.where(lane < DK_D, q, 0.0), jnp.where(lane >= DK_D, q, 0.0)], axis=0).astype(BF16)


def _scores_t(k_tile, q_stack):
    return lax.dot_general(k_tile, q_stack, (((1,), (1,)), ((), ())), preferred_element_type=F32)


def _softmax_t(s, rel, shift, m_old):
    s = s + rel
    m_new = jnp.maximum(m_old, jnp.max(s, axis=0, keepdims=True) + shift)
    p = jnp.exp2(s - (m_new - shift)).astype(BF16)
    return m_new, p, jnp.exp2(m_old - m_new)


def _accumulate_t(acc, alpha, vt_tile, p):
    return alpha * acc + jnp.dot(vt_tile, p, preferred_element_type=F32)


def _attn_finish(acc, lam, g, out_scale, tq):
    o_t = acc[0:DV_D] / acc[DV_D:DV_D + 1]
    o = o_t[:, :tq] - lam * o_t[:, tq:]
    y_t = o * lax.rsqrt(jnp.mean(o * o, axis=0, keepdims=True) + EPS)
    return y_t.T * g * out_scale


def _with_ones(v_t):
    return jnp.concatenate([v_t, jnp.ones((V_ROWS - DV_D, v_t.shape[1]), BF16)], axis=0)


def _attn_prompt_kernel(lam_ref, slope_ref, q_ref, kb_ref, vb_ref, relc_ref, reld_ref, g_ref, o_ref,
                        vt_scr, s_a, s_b, p_a, p_b, m_scr, alpha_scr, acc_scr, *, tq, n_tiles, out_scale):
    h = pl.program_id(1)
    qi = pl.program_id(2)

    @pl.when(qi == 0)
    def _():
        for i in range(n_tiles):
            v_t = vb_ref[0, i * tq:(i + 1) * tq, :].astype(F32).T.astype(BF16)
            vt_scr[i] = _with_ones(v_t)

    slope2 = slope_ref[h] * LOG2E
    q_stack = _stack_maps(q_ref[0])

    def scores(j):
        rows = pl.ds(pl.multiple_of(j * tq, tq), tq)
        return _scores_t(kb_ref[0, rows, :], q_stack)

    def accumulate(j, p):
        acc_scr[...] = _accumulate_t(acc_scr[...], alpha_scr[...], vt_scr[j], p)

    def softmax_cols(s_ref, rel_ref, shift, lo):
        cols = slice(lo, lo + LANES)
        m_new, p, alpha = _softmax_t(s_ref[:, cols], rel_ref[0, :, cols], shift, m_scr[:, cols])
        m_scr[:, cols] = m_new
        alpha_scr[:, cols] = alpha
        return p

    def step(j, s_in, s_out, p_in, p_out):
        s_out[...] = scores(j + 1)
        accumulate(jnp.maximum(j - 1, 0), p_in[...])
        shift = -slope2 * ((qi - j) * tq).astype(F32)
        for lo in range(0, 2 * tq, LANES):
            p_out[:, lo:lo + LANES] = softmax_cols(s_in, relc_ref, shift, lo)

    def last(s_in, p_in):
        accumulate(jnp.maximum(qi - 1, 0), p_in[...])
        p = jnp.concatenate([softmax_cols(s_in, reld_ref, 0.0, lo) for lo in range(0, 2 * tq, LANES)], axis=1)
        acc = _accumulate_t(acc_scr[...], alpha_scr[...], vt_scr[qi], p)
        o_ref[0] = _attn_finish(acc, lam_ref[0], g_ref[...], out_scale, tq)

    acc_scr[...] = jnp.zeros(acc_scr.shape, F32)
    m_scr[...] = jnp.full(m_scr.shape, NEG_INF, F32)
    alpha_scr[...] = jnp.ones(alpha_scr.shape, F32)
    p_b[...] = jnp.zeros(p_b.shape, BF16)
    s_a[...] = scores(0)

    def ctx_tile(j, carry):
        @pl.when(j % 2 == 0)
        def _():
            step(j, s_a, s_b, p_b, p_a)

        @pl.when(j % 2 == 1)
        def _():
            step(j, s_b, s_a, p_a, p_b)

        return carry

    lax.fori_loop(0, qi, ctx_tile, 0)

    @pl.when(qi % 2 == 0)
    def _():
        last(s_a, p_b)

    @pl.when(qi % 2 == 1)
    def _():
        last(s_b, p_a)


def _attn_tables(tk, tq):
    slopes = jnp.asarray([2.0 ** (-8.0 * (i + 1) / H_D) for i in range(H_D)], F32)
    slopes2 = slopes * LOG2E
    key = lax.broadcasted_iota(jnp.int32, (tk, 2 * tq), 0)
    qry = lax.broadcasted_iota(jnp.int32, (tk, 2 * tq), 1) % tq
    rel_ctx = slopes2[:, None, None] * (key - qry).astype(F32)[None]
    key_d = lax.broadcasted_iota(jnp.int32, (tq, 2 * tq), 0)
    qry_d = lax.broadcasted_iota(jnp.int32, (tq, 2 * tq), 1) % tq
    visible = (key_d // CHUNK) <= (qry_d // CHUNK)
    rel_diag = jnp.where(visible[None], -slopes2[:, None, None] * jnp.abs(qry_d - key_d).astype(F32)[None],
                         NEG_INF)
    return slopes, rel_ctx, rel_diag


def _attn_prompt(lam, z3, kb, vb, g, out_scale):
    b, t, _ = z3.shape
    tq = min(ATTN_TILE, t)
    n_tiles = t // tq
    hw = 2 * DK_D
    slopes, rel_ctx, rel_diag = _attn_tables(tq, tq)
    smem = pl.BlockSpec(memory_space=pltpu.SMEM)
    return pl.pallas_call(
        functools.partial(_attn_prompt_kernel, tq=tq, n_tiles=n_tiles, out_scale=out_scale),
        grid=(b, H_D, n_tiles),
        in_specs=[
            smem, smem,
            pl.BlockSpec((1, tq, hw), lambda i, h, j: (i, j, Z_DQ // hw + h)),
            pl.BlockSpec((1, t, hw), lambda i, h, j: (i, 0, h)),
            pl.BlockSpec((1, t, DV_D), lambda i, h, j: (i, 0, h)),
            pl.BlockSpec((1, tq, 2 * tq), lambda i, h, j: (h, 0, 0)),
            pl.BlockSpec((1, tq, 2 * tq), lambda i, h, j: (h, 0, 0)),
            pl.BlockSpec((1, DV_D), lambda i, h, j: (0, 0)),
        ],
        out_specs=pl.BlockSpec((1, tq, DV_D), lambda i, h, j: (i, j, h)),
        out_shape=jax.ShapeDtypeStruct((b, t, W_DIFF), F32),
        scratch_shapes=[
            pltpu.VMEM((n_tiles, V_ROWS, tq), BF16),
            pltpu.VMEM((tq, 2 * tq), F32), pltpu.VMEM((tq, 2 * tq), F32),
            pltpu.VMEM((tq, 2 * tq), BF16), pltpu.VMEM((tq, 2 * tq), BF16),
            pltpu.VMEM((1, 2 * tq), F32), pltpu.VMEM((1, 2 * tq), F32),
            pltpu.VMEM((V_ROWS, 2 * tq), F32),
        ],
        compiler_params=_cparams(("parallel", "parallel", "arbitrary"), 32),
        name="attn_prompt",
    )(lam, slopes, z3, kb, vb, rel_ctx, rel_diag, g)


def _attn_sample_kernel(lam_ref, slope_ref, q_ref, kc_ref, vc_ref, kb_ref, vb_ref, relc_ref, reld_ref, g_ref,
                        o_ref, *, tq, tk, n_ctx, q_off, out_scale):
    hw = 2 * DK_D
    for h in range(H_D):
        slope2 = slope_ref[h] * LOG2E
        q_stack = _stack_maps(q_ref[0, :, h * hw:(h + 1) * hw])
        m = jnp.full((1, 2 * tq), NEG_INF, F32)
        acc = jnp.zeros((V_ROWS, 2 * tq), F32)
        for j in range(n_ctx):
            k_tile = kc_ref[0, j * tk:(j + 1) * tk, h, :].astype(BF16)
            v_t = vc_ref[0, j * tk:(j + 1) * tk, h, :].T.astype(BF16)
            shift = -slope2 * float(q_off - j * tk)
            m, p, alpha = _softmax_t(_scores_t(k_tile, q_stack), relc_ref[h], shift, m)
            acc = _accumulate_t(acc, alpha, _with_ones(v_t), p)
        v_t = vb_ref[0, :, h * hw:(h + 1) * hw].astype(F32).T.astype(BF16)
        s = _scores_t(kb_ref[0, :, h * hw:(h + 1) * hw], q_stack)
        _, p, alpha = _softmax_t(s, reld_ref[h], 0.0, m)
        acc = _accumulate_t(acc, alpha, _with_ones(v_t), p)
        o_ref[0, :, h * hw:(h + 1) * hw] = _attn_finish(acc, lam_ref[0], g_ref[...], out_scale, tq)


def _attn_sample(lam, z3, past_k, past_v, kb, vb, g, out_scale):
    b, t, _ = z3.shape
    t_past = past_k.shape[1]
    tk = min(ATTN_CTX_TILE, t_past)
    slopes, rel_ctx, rel_diag = _attn_tables(tk, t)
    smem = pl.BlockSpec(memory_space=pltpu.SMEM)
    return pl.pallas_call(
        functools.partial(_attn_sample_kernel, tq=t, tk=tk, n_ctx=t_past // tk, q_off=t_past,
                          out_scale=out_scale),
        grid=(b,),
        in_specs=[
            smem, smem,
            pl.BlockSpec((1, t, W_DIFF), lambda i: (i, 0, Z_DQ // W_DIFF)),
            pl.BlockSpec((1, t_past, H_D, 2 * DK_D), lambda i: (i, 0, 0, 0)),
            pl.BlockSpec((1, t_past, H_D, DV_D), lambda i: (i, 0, 0, 0)),
            pl.BlockSpec((1, t, W_DIFF), lambda i: (i, 0, 0)),
            pl.BlockSpec((1, t, W_DIFF), lambda i: (i, 0, 0)),
            _const_spec((H_D, tk, 2 * t)),
            _const_spec((H_D, t, 2 * t)),
            _const_spec((1, DV_D)),
        ],
        out_specs=pl.BlockSpec((1, t, W_DIFF), lambda i: (i, 0, 0)),
        out_shape=jax.ShapeDtypeStruct((b, t, W_DIFF), F32),
        compiler_params=_cparams(("parallel",), 48),
        name="attn_sample",
    )(lam, slopes, z3, past_k, past_v, kb, vb, rel_ctx, rel_diag, g)


def _seg_mean(x, avg):
    hi = x.astype(BF16)
    lo = (x - hi.astype(F32)).astype(BF16)
    return jnp.dot(jnp.concatenate([hi, lo], axis=1), avg, preferred_element_type=F32)


def _outproj_kernel(x_ref, oa_ref, gr_ref, od_ref, or_ref, rg_ref, ga_ref, gret_ref, w_ref, gpost_ref,
                    o_ref, cat_scr):
    lane_r = lax.broadcasted_iota(jnp.int32, (2 * W_GLA, W_GLA), 0) % W_GLA // DV_A
    lane_c = lax.broadcasted_iota(jnp.int32, (2 * W_GLA, W_GLA), 1) // DV_A
    avg = jnp.where(lane_r == lane_c, 1.0 / DV_A, 0.0).astype(BF16)

    oa = oa_ref[...]
    gr = gr_ref[...]
    oa_n = oa * lax.rsqrt(_seg_mean(oa * oa, avg) + EPS) * ga_ref[...]
    cat_scr[:, 0:W_GLA] = (oa_n * (gr * _sigmoid(gr))).astype(BF16)

    cat_scr[:, W_GLA:W_GLA + W_DIFF] = od_ref[...].astype(BF16)

    orr = or_ref[...]
    rg = rg_ref[...]
    cen = orr - _seg_mean(orr, avg)
    or_n = cen * lax.rsqrt(_seg_mean(cen * cen, avg) + EPS) * gret_ref[...]
    cat_scr[:, W_GLA + W_DIFF:] = (or_n * (rg * _sigmoid(rg))).astype(BF16)

    m = jnp.dot(cat_scr[...], w_ref[...], preferred_element_type=F32)
    o_ref[...] = x_ref[...] + _rms(m, gpost_ref[...])


def _outproj(x, oa, z, od, orr, ga, gret, w, gpost):
    n = x.shape[0]
    tm = min(TOKEN_TILE, n)
    return pl.pallas_call(
        _outproj_kernel,
        grid=(n // tm,),
        in_specs=[
            pl.BlockSpec((tm, D_MODEL), lambda i: (i, 0)),
            pl.BlockSpec((tm, W_GLA), lambda i: (i, 0)),
            pl.BlockSpec((tm, W_GLA), lambda i: (i, Z_GR // W_GLA)),
            pl.BlockSpec((tm, W_DIFF), lambda i: (i, 0)),
            pl.BlockSpec((tm, W_RET), lambda i: (i, 0)),
            pl.BlockSpec((tm, W_RET), lambda i: (i, Z_RG // W_RET)),
            _const_spec((1, W_GLA)),
            _const_spec((1, W_RET)),
            _const_spec((D_MODEL, D_MODEL)),
            _const_spec((1, D_MODEL)),
        ],
        out_specs=pl.BlockSpec((tm, D_MODEL), lambda i: (i, 0)),
        out_shape=jax.ShapeDtypeStruct((n, D_MODEL), F32),
        scratch_shapes=[pltpu.VMEM((tm, D_MODEL), BF16)],
        compiler_params=_cparams(("parallel",), 32),
        name="outproj",
    )(x, oa, z, od, orr, z, ga, gret, w, gpost)


def _regroup_w_in(w):
    offs = [0]
    for s in (GLA_RANK, H_A * DK_A, H_A * DK_A, W_GLA, W_GLA, 2 * H_D * DK_D, 2 * H_D * DK_D, W_DIFF,
              H_R * DK_R, H_R * DK_R, W_RET, W_RET):
        offs.append(offs[-1] + s)
    a_lr, gq, gk, gv, gr, dq, dk, dv, rq, rk, rv, rg = (w[:, offs[i]:offs[i + 1]] for i in range(12))
    a_pad = jnp.pad(a_lr, ((0, 0), (0, LANES - GLA_RANK)))
    return jnp.concatenate([gv, gr, rq, rk, rv, rg, dq, gq, gk, a_pad, dk, dv], axis=1).astype(BF16)


def _state_to_blockdiag_t(s):
    b, h, dk, dv = s.shape
    eye = jnp.eye(h, dtype=s.dtype)
    return jnp.einsum('bhdv,hg->bhvgd', s, eye).reshape(b, h * dv, h * dk)


def _blockdiag_t_to_state(st, h, dk, dv):
    b = st.shape[0]
    blocks = st.reshape(b, h, dv, h, dk)
    diag = jnp.stack([blocks[:, i, :, i, :] for i in range(h)], axis=1)
    return diag.swapaxes(-1, -2)


def _layer(x, l, depth, p, past_k, past_v, s_gla0, s_ret0, kv_prev):
    b, t, _ = x.shape
    n = b * t
    x = _ffn(x.reshape(n, D_MODEL), p['ffn1_norm_pre'][l], p['ffn1_w_up'][l], p['ffn1_w_down'][l],
             p['ffn1_norm_post'][l])
    z, k4, v4, kb, vb = _inproj(x, p['mix_norm_pre'][l], p['w_in'][l], p['w_gla_a2'][l], p['b_gla_a'][l],
                                l, depth, kv_prev)
    z3 = z.reshape(b, t, Z_W)
    kb3 = kb.reshape(b, t, W_DIFF)
    vb3 = vb.reshape(b, t, W_DIFF)

    o_a, st_gla = _gla(z3, s_gla0)

    out_scale = 1.0 - (0.8 - 0.6 * math.exp(-0.3 * l))
    if past_k is None:
        o_d = _attn_prompt(p['lam'][l], z3, kb3, vb3, p['diff_subln'][l], out_scale)
        past_len = 0
    else:
        o_d = _attn_sample(p['lam'][l], z3, past_k, past_v, kb3, vb3, p['diff_subln'][l], out_scale)
        past_len = past_k.shape[1]

    o_r, st_ret = _ret(z3, s_ret0, past_len)

    x = _outproj(x, o_a.reshape(n, W_GLA), z, o_d.reshape(n, W_DIFF), o_r.reshape(n, W_RET),
                 p['gla_norm'][l], p['ret_norm'][l], p['w_out'][l], p['mix_norm_post'][l])
    x = _ffn(x, p['ffn2_norm_pre'][l], p['ffn2_w_up'][l], p['ffn2_w_down'][l], p['ffn2_norm_post'][l])
    return x.reshape(b, t, D_MODEL), (k4, v4), st_gla, st_ret


def kernel(x_prompt, x_sample, cache_diff_k, cache_diff_v, state_gla, state_ret, ffn1_norm_pre, ffn1_w_up, ffn1_w_down, ffn1_norm_post, mix_norm_pre, w_in, w_gla_a2, b_gla_a, gla_norm, diff_lambda, diff_subln, ret_norm, w_out, mix_norm_post, ffn2_norm_pre, ffn2_w_up, ffn2_w_down, ffn2_norm_post):
    depth = w_in.shape[0]
    row = lambda a: a.reshape(depth, 1, -1)
    lam_p = diff_lambda.astype(F32)
    lam_init = jnp.asarray([0.8 - 0.6 * math.exp(-0.3 * l) for l in range(depth)], F32)
    lam = (jnp.exp(jnp.sum(lam_p[:, 0] * lam_p[:, 1], axis=-1))
           - jnp.exp(jnp.sum(lam_p[:, 2] * lam_p[:, 3], axis=-1)) + lam_init)
    p = {
        'ffn1_norm_pre': row(ffn1_norm_pre), 'ffn1_w_up': ffn1_w_up.astype(BF16),
        'ffn1_w_down': ffn1_w_down.astype(BF16), 'ffn1_norm_post': row(ffn1_norm_post),
        'mix_norm_pre': row(mix_norm_pre),
        'w_in': jnp.stack([_regroup_w_in(w_in[l]) for l in range(depth)]),
        'w_gla_a2': jnp.pad(w_gla_a2, ((0, 0), (0, LANES - GLA_RANK), (0, 0))).astype(BF16),
        'b_gla_a': row(b_gla_a), 'gla_norm': row(gla_norm), 'lam': lam.reshape(depth, 1),
        'diff_subln': row(diff_subln), 'ret_norm': row(ret_norm), 'w_out': w_out.astype(BF16),
        'mix_norm_post': row(mix_norm_post),
        'ffn2_norm_pre': row(ffn2_norm_pre), 'ffn2_w_up': ffn2_w_up.astype(BF16),
        'ffn2_w_down': ffn2_w_down.astype(BF16), 'ffn2_norm_post': row(ffn2_norm_post),
    }
    bp = x_prompt.shape[0]
    bs = x_sample.shape[0]
    zeros_gla = jnp.zeros((bp, W_GLA, H_A * DK_A), F32)
    zeros_ret = jnp.zeros((bp, W_RET, H_R * DK_R), F32)
    xp, xs = x_prompt, x_sample
    kv_p = kv_s = None
    st_p, st_s = [], []
    for l in range(depth):
        xp, kv_p, g_, r_ = _layer(xp, l, depth, p, None, None, zeros_gla, zeros_ret, kv_p)
        st_p.append((g_, r_))
        xs, kv_s, g_, r_ = _layer(xs, l, depth, p, cache_diff_k[l], cache_diff_v[l],
                                  _state_to_blockdiag_t(state_gla[l]), _state_to_blockdiag_t(state_ret[l]), kv_s)
        st_s.append((g_, r_))

    def states(sts):
        g = jnp.stack([_blockdiag_t_to_state(s[0], H_A, DK_A, DV_A) for s in sts])
        r = jnp.stack([_blockdiag_t_to_state(s[1], H_R, DK_R, DV_R) for s in sts])
        return g, r

    gp, rp = states(st_p)
    gs, rs = states(st_s)
    kv4 = lambda a, b: a.reshape(depth, b, -1, H_D, 2 * DK_D)
    return (xp, xs, kv4(kv_p[0], bp), kv4(kv_p[1], bp), gp, rp, kv4(kv_s[0], bs), kv4(kv_s[1], bs), gs, rs)
```

```python
import functools
import math

import jax
import jax.numpy as jnp
from jax import lax
from jax.experimental import pallas as pl
from jax.experimental.pallas import tpu as pltpu

F32 = jnp.float32
BF16 = jnp.bfloat16

D_MODEL = 1024
D_FF = 2816
EPS = 1e-6
NEG_INF = -1e30
CHUNK = 64

H_A, DK_A, DV_A = 4, 32, 64
GLA_RANK = 16
GLA_TAU = 16.0
H_D, DK_D, DV_D = 4, 64, 128
H_R, DK_R, DV_R = 4, 64, 64
ROPE_BASE = 10000.0

W_GLA = H_A * DV_A
W_DIFF = H_D * DV_D
W_RET = H_R * DV_R

LANES = 128
SUBLANES = 8

WP_DK = 1280
WP_DV = 1792
WP_RET = 2304
WP_ALR = 3328
WP_W = 3456
Z_GQ, Z_GK, Z_GV, Z_GR, Z_DQ = 0, 128, 256, 512, 768
Z_RQ, Z_RK, Z_RV, Z_RG = 1280, 1536, 1792, 2048
Z_LA = 2304
Z_W = 2432

TOKEN_TILE = 512
FF_COLS = 256
SCAN_TILE = 512
RET_CHUNK = 128
GLA_SUB = 16
ATTN_TILE = 256
ATTN_CTX_TILE = 512

_MIB = 1024 * 1024


def _cparams(sem, vmem_mib):
    return pltpu.CompilerParams(dimension_semantics=sem, vmem_limit_bytes=vmem_mib * _MIB)


def _rms(x, g):
    return x * lax.rsqrt(jnp.mean(x * x, axis=-1, keepdims=True) + EPS) * g


def _sigmoid(x):
    return 1.0 / (1.0 + jnp.exp(-x))


def _const_spec(shape):
    n = len(shape)
    return pl.BlockSpec(shape, lambda *_: (0,) * n, pipeline_mode=pl.Buffered(1))


def _ffn_kernel(x_ref, gpre_ref, wup_ref, wdown_ref, gpost_ref, o_ref, h_scr, act_scr):
    x = x_ref[...]
    h_scr[...] = _rms(x, gpre_ref[...]).astype(BF16)
    for c in range(D_FF // FF_COLS):
        lo = c * FF_COLS
        gate = jnp.dot(h_scr[...], wup_ref[:, lo:lo + FF_COLS], preferred_element_type=F32)
        up = jnp.dot(h_scr[...], wup_ref[:, D_FF + lo:D_FF + lo + FF_COLS], preferred_element_type=F32)
        act_scr[:, lo:lo + FF_COLS] = (gate * _sigmoid(gate) * up).astype(BF16)
    y = jnp.dot(act_scr[...], wdown_ref[...], preferred_element_type=F32)
    o_ref[...] = x + 0.5 * _rms(y, gpost_ref[...])


def _ffn(x, gpre, wup, wdown, gpost):
    n = x.shape[0]
    tm = min(TOKEN_TILE, n)
    return pl.pallas_call(
        _ffn_kernel,
        grid=(n // tm,),
        in_specs=[
            pl.BlockSpec((tm, D_MODEL), lambda i: (i, 0)),
            _const_spec((1, D_MODEL)),
            _const_spec((D_MODEL, 2 * D_FF)),
            _const_spec((D_FF, D_MODEL)),
            _const_spec((1, D_MODEL)),
        ],
        out_specs=pl.BlockSpec((tm, D_MODEL), lambda i: (i, 0)),
        out_shape=jax.ShapeDtypeStruct((n, D_MODEL), F32),
        scratch_shapes=[pltpu.VMEM((tm, D_MODEL), BF16), pltpu.VMEM((tm, D_FF), BF16)],
        compiler_params=_cparams(("parallel",), 48),
        name="ffn",
    )(x, gpre, wup, wdown, gpost)


def _inproj_kernel(*refs, n_alias):
    x_ref, g_ref, w_ref, wa2_ref, ba_ref = refs[:5]
    z_ref, k4_ref, v4_ref, kb_ref, vb_ref, h_scr = refs[5 + n_alias:]
    h_scr[...] = _rms(x_ref[...], g_ref[...]).astype(BF16)
    for lo in range(0, WP_DK, 256):
        z_ref[:, lo:lo + 256] = jnp.dot(h_scr[...], w_ref[:, lo:lo + 256], preferred_element_type=F32)
    for lo in range(0, WP_ALR - WP_RET, 256):
        z_ref[:, Z_RQ + lo:Z_RQ + lo + 256] = jnp.dot(
            h_scr[...], w_ref[:, WP_RET + lo:WP_RET + lo + 256], preferred_element_type=F32)
    a_lr = jnp.dot(h_scr[...], w_ref[:, WP_ALR:WP_ALR + LANES], preferred_element_type=F32)
    pre = jnp.dot(a_lr.astype(BF16), wa2_ref[...], preferred_element_type=F32) + ba_ref[...]
    log_sig = jnp.minimum(pre, 0.0) - jnp.log(1.0 + jnp.exp(-jnp.abs(pre)))
    z_ref[:, Z_LA:Z_LA + LANES] = log_sig / GLA_TAU
    hw = 2 * DK_D
    for h in range(H_D):
        k_h = jnp.dot(h_scr[...], w_ref[:, WP_DK + h * hw:WP_DK + (h + 1) * hw], preferred_element_type=F32)
        v_h = jnp.dot(h_scr[...], w_ref[:, WP_DV + h * hw:WP_DV + (h + 1) * hw], preferred_element_type=F32)
        k4_ref[0, :, h, :] = k_h
        v4_ref[0, :, h, :] = v_h
        kb_ref[:, h * hw:(h + 1) * hw] = k_h.astype(BF16)
        vb_ref[:, h * hw:(h + 1) * hw] = v_h.astype(BF16)


def _inproj(x, g, w, wa2, ba, layer, depth, kv_prev):
    n = x.shape[0]
    tm = min(TOKEN_TILE, n)
    hw = 2 * DK_D
    n_alias = 0 if kv_prev is None else 2
    kv_shape = jax.ShapeDtypeStruct((depth, n, H_D, hw), F32)
    kv_spec = pl.BlockSpec((1, tm, H_D, hw), lambda i: (layer, i, 0, 0))
    return pl.pallas_call(
        functools.partial(_inproj_kernel, n_alias=n_alias),
        grid=(n // tm,),
        in_specs=[
            pl.BlockSpec((tm, D_MODEL), lambda i: (i, 0)),
            _const_spec((1, D_MODEL)),
            _const_spec((D_MODEL, WP_W)),
            _const_spec((LANES, LANES)),
            _const_spec((1, LANES)),
        ] + [pl.BlockSpec(memory_space=pl.ANY)] * n_alias,
        out_specs=[
            pl.BlockSpec((tm, Z_W), lambda i: (i, 0)),
            kv_spec, kv_spec,
            pl.BlockSpec((tm, W_DIFF), lambda i: (i, 0)),
            pl.BlockSpec((tm, W_DIFF), lambda i: (i, 0)),
        ],
        out_shape=[
            jax.ShapeDtypeStruct((n, Z_W), F32),
            kv_shape, kv_shape,
            jax.ShapeDtypeStruct((n, W_DIFF), BF16),
            jax.ShapeDtypeStruct((n, W_DIFF), BF16),
        ],
        input_output_aliases={} if kv_prev is None else {5: 1, 6: 2},
        scratch_shapes=[pltpu.VMEM((tm, D_MODEL), BF16)],
        compiler_params=_cparams(("parallel",), 48),
        name="inproj",
    )(x, g, w, wa2, ba, *(() if kv_prev is None else kv_prev))


def _gla_kernel(q_ref, k_ref, v_ref, la_ref, s0_ref, o_ref, st_ref, st_scr, *, n_chunks):
    c = CHUNK

    @pl.when(pl.program_id(1) == 0)
    def _():
        st_scr[...] = s0_ref[0]

    r_i = lax.broadcasted_iota(jnp.int32, (c, c), 0)
    c_i = lax.broadcasted_iota(jnp.int32, (c, c), 1)
    tri = (r_i >= c_i).astype(BF16)
    head_k = lax.broadcasted_iota(jnp.int32, (H_A * DK_A, W_GLA), 0) // DK_A
    head_v = lax.broadcasted_iota(jnp.int32, (H_A * DK_A, W_GLA), 1) // DV_A
    sum_bcast = (head_k == head_v).astype(BF16)
    st_hv = lax.broadcasted_iota(jnp.int32, (W_GLA, H_A * DK_A), 0) // DV_A
    st_hk = lax.broadcasted_iota(jnp.int32, (W_GLA, H_A * DK_A), 1) // DK_A
    st_mask = st_hv == st_hk
    sub = lax.broadcasted_iota(jnp.int32, (SUBLANES, H_A * DK_A), 0)
    n_groups = c // SUBLANES
    sub_groups = GLA_SUB // SUBLANES
    head_q = lax.broadcasted_iota(jnp.int32, (GLA_SUB, H_A * DK_A), 1) // DK_A
    head_o = lax.broadcasted_iota(jnp.int32, (GLA_SUB, W_GLA), 1) // DV_A

    def chunk(ci, st):
        rows = slice(ci * c, (ci + 1) * c)
        q = q_ref[0, rows, :] * (DK_A ** -0.5)
        k = k_ref[0, rows, :]
        v = v_ref[0, rows, :]
        la = la_ref[0, rows, :]

        la1 = la.astype(BF16)
        rem = la - la1.astype(F32)
        la2 = rem.astype(BF16)
        la3 = (rem - la2.astype(F32)).astype(BF16)
        b3 = jnp.dot(tri, jnp.concatenate([la1, la2, la3], axis=1), preferred_element_type=F32)
        b = b3[:, :LANES] + b3[:, LANES:2 * LANES] + b3[:, 2 * LANES:]

        pieces = []
        for s in range(c):
            blk_end = (s // GLA_SUB + 1) * GLA_SUB
            r0 = (s // SUBLANES) * SUBLANES
            e = jnp.exp(jnp.minimum(b[r0:blk_end, :] - b[s:s + 1, :], 0.0))
            p = q[r0:blk_end, :] * e * k[s:s + 1, :]
            if s % SUBLANES:
                first = jnp.where(sub >= (s % SUBLANES), p[:SUBLANES], 0.0)
                p = first if blk_end - r0 == SUBLANES else jnp.concatenate([first, p[SUBLANES:]], axis=0)
            pieces.append(p)
        p_all = jnp.concatenate(pieces, axis=0).astype(BF16)
        w_all = jnp.dot(p_all, sum_bcast, preferred_element_type=F32)
        acc = [None] * n_groups
        off = 0
        for s in range(c):
            v_s = v[s:s + 1, :]
            for g in range(s // SUBLANES, (s // GLA_SUB + 1) * GLA_SUB // SUBLANES):
                t = w_all[off:off + SUBLANES, :] * v_s
                acc[g] = t if acc[g] is None else acc[g] + t
                off += SUBLANES
        o_blocks = [jnp.concatenate(acc[i * sub_groups:(i + 1) * sub_groups], axis=0)
                    for i in range(c // GLA_SUB)]
        v_bf = v.astype(BF16)
        for i in range(1, c // GLA_SUB):
            lo = i * GLA_SUB
            ref = b[lo - 1:lo, :]
            q_i = q[lo:lo + GLA_SUB, :] * jnp.exp(b[lo:lo + GLA_SUB, :] - ref)
            k_i = (k[:lo, :] * jnp.exp(ref - b[:lo, :])).astype(BF16)
            q_heads = jnp.concatenate(
                [jnp.where(head_q == hh, q_i, 0.0) for hh in range(H_A)], axis=0).astype(BF16)
            a_i = lax.dot_general(q_heads, k_i, (((1,), (1,)), ((), ())), preferred_element_type=F32)
            r_i = jnp.dot(a_i.astype(BF16), v_bf[:lo, :], preferred_element_type=F32)
            for hh in range(H_A):
                o_blocks[i] = o_blocks[i] + jnp.where(head_o == hh, r_i[hh * GLA_SUB:(hh + 1) * GLA_SUB], 0.0)
        o_intra = jnp.concatenate(o_blocks, axis=0)

        q_dec = (q * jnp.exp(b)).astype(BF16)
        o_inter = lax.dot_general(q_dec, st.astype(BF16), (((1,), (1,)), ((), ())),
                                  preferred_element_type=F32)
        b_last = b[c - 1:c, :]
        k_dec = (k * jnp.exp(b_last - b)).astype(BF16)
        upd = lax.dot_general(v_bf, k_dec, (((0,), (0,)), ((), ())),
                              preferred_element_type=F32)
        return st * jnp.exp(b_last) + jnp.where(st_mask, upd, 0.0), o_intra + o_inter

    st = st_scr[...]
    outs = []
    for ci in range(n_chunks):
        st, o = chunk(ci, st)
        outs.append(o)
    for ci in range(n_chunks):
        o_ref[0, ci * c:(ci + 1) * c, :] = outs[ci]
    st_scr[...] = st
    st_ref[0] = st


def _gla(z, s0t):
    b, t, _ = z.shape
    tb = min(SCAN_TILE, t)
    kw = H_A * DK_A
    return pl.pallas_call(
        functools.partial(_gla_kernel, n_chunks=tb // CHUNK),
        grid=(b, t // tb),
        in_specs=[
            pl.BlockSpec((1, tb, kw), lambda i, j: (i, j, Z_GQ // kw)),
            pl.BlockSpec((1, tb, kw), lambda i, j: (i, j, Z_GK // kw)),
            pl.BlockSpec((1, tb, W_GLA), lambda i, j: (i, j, Z_GV // W_GLA)),
            pl.BlockSpec((1, tb, kw), lambda i, j: (i, j, Z_LA // kw)),
            pl.BlockSpec((1, W_GLA, kw), lambda i, j: (i, 0, 0)),
        ],
        out_specs=[
            pl.BlockSpec((1, tb, W_GLA), lambda i, j: (i, j, 0)),
            pl.BlockSpec((1, W_GLA, kw), lambda i, j: (i, 0, 0)),
        ],
        out_shape=[
            jax.ShapeDtypeStruct((b, t, W_GLA), F32),
            jax.ShapeDtypeStruct((b, W_GLA, kw), F32),
        ],
        scratch_shapes=[pltpu.VMEM((W_GLA, kw), F32)],
        compiler_params=_cparams(("parallel", "arbitrary"), 32),
        name="gla",
    )(z, z, z, z, s0t)


def _swap_halves(x):
    lane = lax.broadcasted_iota(jnp.int32, (x.shape[0], LANES), 1)
    first_half = (lane % DK_R) < (DK_R // 2)
    cols = []
    for lo in range(0, x.shape[1], LANES):
        blk = x[:, lo:lo + LANES]
        up = pltpu.roll(blk, LANES - DK_R // 2, axis=1)
        down = pltpu.roll(blk, DK_R // 2, axis=1)
        cols.append(jnp.where(first_half, up, down))
    return jnp.concatenate(cols, axis=1)


def _ret_kernel(q_ref, k_ref, v_ref, cos_ref, sin_ref, dec_ref, qdec_ref, kdec_ref, cdec_ref, s0_ref,
                o_ref, st_ref, st_scr, *, n_chunks, c):
    @pl.when(pl.program_id(1) == 0)
    def _():
        st_scr[...] = s0_ref[0]

    head_l = lax.broadcasted_iota(jnp.int32, (c, W_RET), 1) // DK_R
    st_mask = (lax.broadcasted_iota(jnp.int32, (W_RET, W_RET), 0) // DV_R
               == lax.broadcasted_iota(jnp.int32, (W_RET, W_RET), 1) // DK_R)

    def chunk(ci, carry):
        rows = pl.ds(pl.multiple_of(ci * c, c), c)
        cos = cos_ref[rows, :]
        sin = sin_ref[rows, :]
        q = q_ref[0, rows, :]
        k = k_ref[0, rows, :]
        v = v_ref[0, rows, :].astype(BF16)
        q = q * cos + _swap_halves(q) * sin
        k = (k * cos + _swap_halves(k) * sin) * (DK_R ** -0.5)

        q_stack = jnp.concatenate(
            [jnp.where(head_l == h, q, 0.0) for h in range(H_R)], axis=0).astype(BF16)
        s = lax.dot_general(q_stack, k.astype(BF16), (((1,), (1,)), ((), ())),
                            preferred_element_type=F32)
        s = (s * dec_ref[...]).astype(BF16)
        r = jnp.dot(s, v, preferred_element_type=F32)
        o = jnp.where(head_l == 0, r[:c], 0.0)
        for h in range(1, H_R):
            o = o + jnp.where(head_l == h, r[h * c:(h + 1) * c], 0.0)

        st = st_scr[...]
        o = o + lax.dot_general((q * qdec_ref[...]).astype(BF16), st.astype(BF16),
                                (((1,), (1,)), ((), ())), preferred_element_type=F32)
        upd = lax.dot_general(v, (k * kdec_ref[...]).astype(BF16), (((0,), (0,)), ((), ())),
                              preferred_element_type=F32)
        st_scr[...] = st * cdec_ref[...] + jnp.where(st_mask, upd, 0.0)
        o_ref[0, rows, :] = o
        return carry

    lax.fori_loop(0, n_chunks, chunk, 0)
    st_ref[0] = st_scr[...]


def _ret_tables(c):
    lg = jnp.log(1.0 - 2.0 ** (-5.0 - jnp.arange(H_R, dtype=F32)))
    idx = jnp.arange(c, dtype=F32)
    tri = idx[:, None] >= idx[None, :]
    dec = jnp.exp(jnp.where(tri[None], (idx[:, None] - idx[None, :])[None] * lg[:, None, None], -jnp.inf))
    q_dec = jnp.exp((idx[None, :] + 1.0) * lg[:, None])
    k_dec = jnp.exp((c - 1.0 - idx[None, :]) * lg[:, None])
    c_dec = jnp.exp(c * lg)
    return (dec.reshape(H_R * c, c),
            jnp.repeat(q_dec.T, DK_R, axis=1),
            jnp.repeat(k_dec.T, DK_R, axis=1),
            jnp.repeat(c_dec, DK_R)[None, :])


def _rope_tables(pos):
    half = DK_R // 2
    inv = ROPE_BASE ** (-jnp.arange(half, dtype=F32) / half)
    ang = pos.astype(F32)[:, None] * inv[None, :]
    cos = jnp.cos(ang)
    sin = jnp.sin(ang)
    cos_t = jnp.tile(jnp.concatenate([cos, cos], axis=1), (1, H_R))
    sin_t = jnp.tile(jnp.concatenate([-sin, sin], axis=1), (1, H_R))
    return cos_t, sin_t


def _ret(z, s0t, past_len):
    b, t, _ = z.shape
    tb = min(SCAN_TILE, t)
    c = min(RET_CHUNK, t)
    cos_t, sin_t = _rope_tables(past_len + jnp.arange(t))
    dec, q_dec, k_dec, c_dec = _ret_tables(c)
    return pl.pallas_call(
        functools.partial(_ret_kernel, n_chunks=tb // c, c=c),
        grid=(b, t // tb),
        in_specs=[
            pl.BlockSpec((1, tb, W_RET), lambda i, j: (i, j, Z_RQ // W_RET)),
            pl.BlockSpec((1, tb, W_RET), lambda i, j: (i, j, Z_RK // W_RET)),
            pl.BlockSpec((1, tb, W_RET), lambda i, j: (i, j, Z_RV // W_RET)),
            pl.BlockSpec((tb, W_RET), lambda i, j: (j, 0)),
            pl.BlockSpec((tb, W_RET), lambda i, j: (j, 0)),
            _const_spec((H_R * c, c)),
            _const_spec((c, W_RET)),
            _const_spec((c, W_RET)),
            _const_spec((1, W_RET)),
            pl.BlockSpec((1, W_RET, W_RET), lambda i, j: (i, 0, 0)),
        ],
        out_specs=[
            pl.BlockSpec((1, tb, W_RET), lambda i, j: (i, j, 0)),
            pl.BlockSpec((1, W_RET, W_RET), lambda i, j: (i, 0, 0)),
        ],
        out_shape=[
            jax.ShapeDtypeStruct((b, t, W_RET), F32),
            jax.ShapeDtypeStruct((b, W_RET, W_RET), F32),
        ],
        scratch_shapes=[pltpu.VMEM((W_RET, W_RET), F32)],
        compiler_params=_cparams(("parallel", "arbitrary"), 32),
        name="ret",
    )(z, z, z, cos_t, sin_t, dec, q_dec, k_dec, c_dec, s0t)


V_ROWS = DV_D + 16


LOG2E = 1.4426950408889634


def _stack_maps(q):
    lane = lax.broadcasted_iota(jnp.int32, q.shape, 1)
    q = q * (DK_D ** -0.5 * LOG2E)
    return jnp.concatenate(
        [jnp.where(lane < DK_D, q, 0.0), jnp.where(lane >= DK_D, q, 0.0)], axis=0).astype(BF16)


def _scores_t(k_tile, q_stack):
    return lax.dot_general(k_tile, q_stack, (((1,), (1,)), ((), ())), preferred_element_type=F32)


def _softmax_t(s, rel, shift, m_old):
    s = s + rel
    m_new = jnp.maximum(m_old, jnp.max(s, axis=0, keepdims=True) + shift)
    p = jnp.exp2(s - (m_new - shift)).astype(BF16)
    return m_new, p, jnp.exp2(m_old - m_new)


def _accumulate_t(acc, alpha, vt_tile, p):
    return alpha * acc + jnp.dot(vt_tile, p, preferred_element_type=F32)


def _attn_finish(acc, lam, g, out_scale, tq):
    o_t = acc[0:DV_D] / acc[DV_D:DV_D + 1]
    o = o_t[:, :tq] - lam * o_t[:, tq:]
    y_t = o * lax.rsqrt(jnp.mean(o * o, axis=0, keepdims=True) + EPS)
    return y_t.T * g * out_scale


def _with_ones(v_t):
    return jnp.concatenate([v_t, jnp.ones((V_ROWS - DV_D, v_t.shape[1]), BF16)], axis=0)


def _attn_prompt_kernel(lam_ref, slope_ref, q_ref, kb_ref, vb_ref, relc_ref, reld_ref, g_ref, o_ref,
                        vt_scr, s_a, s_b, p_a, p_b, acc_scr, *, tq, n_tiles, out_scale):
    h = pl.program_id(1)
    for i in range(n_tiles):
        v_t = vb_ref[0, i * tq:(i + 1) * tq, :].astype(F32).T.astype(BF16)
        vt_scr[i] = _with_ones(v_t)
    slope2 = slope_ref[h] * LOG2E
    s_bufs = (s_a, s_b)
    p_bufs = (p_a, p_b)
    col_groups = range(0, 2 * tq, LANES)

    def softmax_all(s_ref, rel_ref, shift, m_old):
        return [_softmax_t(s_ref[:, lo:lo + LANES], rel_ref[0, :, lo:lo + LANES], shift, m_old[i])
                for i, lo in enumerate(col_groups)]

    for qi in range(n_tiles):
        q_stack = _stack_maps(q_ref[0, qi * tq:(qi + 1) * tq, :])

        def scores(j):
            return _scores_t(kb_ref[0, j * tq:(j + 1) * tq, :], q_stack)

        s_bufs[0][...] = scores(0)
        m = [jnp.full((1, LANES), NEG_INF, F32) for _ in col_groups]
        alpha_prev = None
        for j in range(qi):
            s_in, s_out = s_bufs[j % 2], s_bufs[1 - j % 2]
            p_in, p_out = p_bufs[1 - j % 2], p_bufs[j % 2]
            stats = softmax_all(s_in, relc_ref, -slope2 * float((qi - j) * tq), m)
            s_next = scores(j + 1)
            if j == 1:
                acc_new = jnp.dot(vt_scr[0], p_in[...], preferred_element_type=F32)
            elif j > 1:
                acc_new = _accumulate_t(acc_scr[...], alpha_prev, vt_scr[j - 1], p_in[...])
            for (m_new, p, alpha), lo in zip(stats, col_groups):
                p_out[:, lo:lo + LANES] = p
            m = [st[0] for st in stats]
            alpha_prev = jnp.concatenate([st[2] for st in stats], axis=1)
            s_out[...] = s_next
            if j >= 1:
                acc_scr[...] = acc_new
        stats = softmax_all(s_bufs[qi % 2], reld_ref, 0.0, m)
        p = jnp.concatenate([st[1] for st in stats], axis=1)
        alpha = jnp.concatenate([st[2] for st in stats], axis=1)
        if qi == 0:
            acc = jnp.dot(vt_scr[0], p, preferred_element_type=F32)
        else:
            if qi == 1:
                acc = jnp.dot(vt_scr[0], p_bufs[1 - qi % 2][...], preferred_element_type=F32)
            else:
                acc = _accumulate_t(acc_scr[...], alpha_prev, vt_scr[qi - 1], p_bufs[1 - qi % 2][...])
            acc = _accumulate_t(acc, alpha, vt_scr[qi], p)
        o_ref[0, qi * tq:(qi + 1) * tq, :] = _attn_finish(acc, lam_ref[0], g_ref[...], out_scale, tq)


def _attn_tables(tk, tq):
    slopes = jnp.asarray([2.0 ** (-8.0 * (i + 1) / H_D) for i in range(H_D)], F32)
    slopes2 = slopes * LOG2E
    key = lax.broadcasted_iota(jnp.int32, (tk, 2 * tq), 0)
    qry = lax.broadcasted_iota(jnp.int32, (tk, 2 * tq), 1) % tq
    rel_ctx = slopes2[:, None, None] * (key - qry).astype(F32)[None]
    key_d = lax.broadcasted_iota(jnp.int32, (tq, 2 * tq), 0)
    qry_d = lax.broadcasted_iota(jnp.int32, (tq, 2 * tq), 1) % tq
    visible = (key_d // CHUNK) <= (qry_d // CHUNK)
    rel_diag = jnp.where(visible[None], -slopes2[:, None, None] * jnp.abs(qry_d - key_d).astype(F32)[None],
                         NEG_INF)
    return slopes, rel_ctx, rel_diag


def _attn_prompt(lam, z3, kb, vb, g, out_scale):
    b, t, _ = z3.shape
    tq = min(ATTN_TILE, t)
    n_tiles = t // tq
    hw = 2 * DK_D
    slopes, rel_ctx, rel_diag = _attn_tables(tq, tq)
    smem = pl.BlockSpec(memory_space=pltpu.SMEM)
    return pl.pallas_call(
        functools.partial(_attn_prompt_kernel, tq=tq, n_tiles=n_tiles, out_scale=out_scale),
        grid=(b, H_D),
        in_specs=[
            smem, smem,
            pl.BlockSpec((1, t, hw), lambda i, h: (i, 0, Z_DQ // hw + h)),
            pl.BlockSpec((1, t, hw), lambda i, h: (i, 0, h)),
            pl.BlockSpec((1, t, DV_D), lambda i, h: (i, 0, h)),
            pl.BlockSpec((1, tq, 2 * tq), lambda i, h: (h, 0, 0)),
            pl.BlockSpec((1, tq, 2 * tq), lambda i, h: (h, 0, 0)),
            pl.BlockSpec((1, DV_D), lambda i, h: (0, 0)),
        ],
        out_specs=pl.BlockSpec((1, t, DV_D), lambda i, h: (i, 0, h)),
        out_shape=jax.ShapeDtypeStruct((b, t, W_DIFF), F32),
        scratch_shapes=[
            pltpu.VMEM((n_tiles, V_ROWS, tq), BF16),
            pltpu.VMEM((tq, 2 * tq), F32), pltpu.VMEM((tq, 2 * tq), F32),
            pltpu.VMEM((tq, 2 * tq), BF16), pltpu.VMEM((tq, 2 * tq), BF16),
            pltpu.VMEM((V_ROWS, 2 * tq), F32),
        ],
        compiler_params=_cparams(("parallel", "parallel"), 32),
        name="attn_prompt",
    )(lam, slopes, z3, kb, vb, rel_ctx, rel_diag, g)


def _attn_sample_kernel(lam_ref, slope_ref, q01_ref, q23_ref, kc_ref, vc_ref, kb_ref, vb_ref, relc_ref, reld_ref,
                        g_ref, o_ref, *, tq, tk, n_ctx, q_off, out_scale):
    hw = 2 * DK_D
    for h in range(H_D):
        slope2 = slope_ref[h] * LOG2E
        q_ref = q01_ref if h < 2 else q23_ref
        q_stack = _stack_maps(q_ref[0, :, (h % 2) * hw:(h % 2 + 1) * hw])
        m = jnp.full((1, 2 * tq), NEG_INF, F32)
        acc = jnp.zeros((V_ROWS, 2 * tq), F32)
        for j in range(n_ctx):
            k_tile = kc_ref[0, j * tk:(j + 1) * tk, h, :].astype(BF16)
            v_t = vc_ref[0, j * tk:(j + 1) * tk, h, :].T.astype(BF16)
            shift = -slope2 * float(q_off - j * tk)
            m, p, alpha = _softmax_t(_scores_t(k_tile, q_stack), relc_ref[h], shift, m)
            acc = _accumulate_t(acc, alpha, _with_ones(v_t), p)
        v_t = vb_ref[0, :, h * hw:(h + 1) * hw].astype(F32).T.astype(BF16)
        s = _scores_t(kb_ref[0, :, h * hw:(h + 1) * hw], q_stack)
        _, p, alpha = _softmax_t(s, reld_ref[h], 0.0, m)
        acc = _accumulate_t(acc, alpha, _with_ones(v_t), p)
        o_ref[0, :, h * hw:(h + 1) * hw] = _attn_finish(acc, lam_ref[0], g_ref[...], out_scale, tq)


def _attn_sample(lam, z3, past_k, past_v, kb, vb, g, out_scale):
    b, t, _ = z3.shape
    t_past = past_k.shape[1]
    tk = min(ATTN_CTX_TILE, t_past)
    slopes, rel_ctx, rel_diag = _attn_tables(tk, t)
    smem = pl.BlockSpec(memory_space=pltpu.SMEM)
    return pl.pallas_call(
        functools.partial(_attn_sample_kernel, tq=t, tk=tk, n_ctx=t_past // tk, q_off=t_past,
                          out_scale=out_scale),
        grid=(b,),
        in_specs=[
            smem, smem,
            pl.BlockSpec((1, t, W_DIFF // 2), lambda i: (i, 0, Z_DQ // (W_DIFF // 2))),
            pl.BlockSpec((1, t, W_DIFF // 2), lambda i: (i, 0, Z_DQ // (W_DIFF // 2) + 1)),
            pl.BlockSpec((1, t_past, H_D, 2 * DK_D), lambda i: (i, 0, 0, 0)),
            pl.BlockSpec((1, t_past, H_D, DV_D), lambda i: (i, 0, 0, 0)),
            pl.BlockSpec((1, t, W_DIFF), lambda i: (i, 0, 0)),
            pl.BlockSpec((1, t, W_DIFF), lambda i: (i, 0, 0)),
            _const_spec((H_D, tk, 2 * t)),
            _const_spec((H_D, t, 2 * t)),
            _const_spec((1, DV_D)),
        ],
        out_specs=pl.BlockSpec((1, t, W_DIFF), lambda i: (i, 0, 0)),
        out_shape=jax.ShapeDtypeStruct((b, t, W_DIFF), F32),
        compiler_params=_cparams(("parallel",), 48),
        name="attn_sample",
    )(lam, slopes, z3, z3, past_k, past_v, kb, vb, rel_ctx, rel_diag, g)


def _seg_mean(x, avg):
    hi = x.astype(BF16)
    lo = (x - hi.astype(F32)).astype(BF16)
    return jnp.dot(jnp.concatenate([hi, lo], axis=1), avg, preferred_element_type=F32)


def _outproj_kernel(x_ref, oa_ref, gr_ref, od_ref, or_ref, rg_ref, ga_ref, gret_ref, w_ref, gpost_ref,
                    o_ref, cat_scr):
    lane_r = lax.broadcasted_iota(jnp.int32, (2 * W_GLA, W_GLA), 0) % W_GLA // DV_A
    lane_c = lax.broadcasted_iota(jnp.int32, (2 * W_GLA, W_GLA), 1) // DV_A
    avg = jnp.where(lane_r == lane_c, 1.0 / DV_A, 0.0).astype(BF16)

    oa = oa_ref[...]
    gr = gr_ref[...]
    oa_n = oa * lax.rsqrt(_seg_mean(oa * oa, avg) + EPS) * ga_ref[...]
    cat_scr[:, 0:W_GLA] = (oa_n * (gr * _sigmoid(gr))).astype(BF16)

    cat_scr[:, W_GLA:W_GLA + W_DIFF] = od_ref[...].astype(BF16)

    orr = or_ref[...]
    rg = rg_ref[...]
    cen = orr - _seg_mean(orr, avg)
    or_n = cen * lax.rsqrt(_seg_mean(cen * cen, avg) + EPS) * gret_ref[...]
    cat_scr[:, W_GLA + W_DIFF:] = (or_n * (rg * _sigmoid(rg))).astype(BF16)

    m = jnp.dot(cat_scr[...], w_ref[...], preferred_element_type=F32)
    o_ref[...] = x_ref[...] + _rms(m, gpost_ref[...])


def _outproj(x, oa, z, od, orr, ga, gret, w, gpost):
    n = x.shape[0]
    tm = min(TOKEN_TILE, n)
    return pl.pallas_call(
        _outproj_kernel,
        grid=(n // tm,),
        in_specs=[
            pl.BlockSpec((tm, D_MODEL), lambda i: (i, 0)),
            pl.BlockSpec((tm, W_GLA), lambda i: (i, 0)),
            pl.BlockSpec((tm, W_GLA), lambda i: (i, Z_GR // W_GLA)),
            pl.BlockSpec((tm, W_DIFF), lambda i: (i, 0)),
            pl.BlockSpec((tm, W_RET), lambda i: (i, 0)),
            pl.BlockSpec((tm, W_RET), lambda i: (i, Z_RG // W_RET)),
            _const_spec((1, W_GLA)),
            _const_spec((1, W_RET)),
            _const_spec((D_MODEL, D_MODEL)),
            _const_spec((1, D_MODEL)),
        ],
        out_specs=pl.BlockSpec((tm, D_MODEL), lambda i: (i, 0)),
        out_shape=jax.ShapeDtypeStruct((n, D_MODEL), F32),
        scratch_shapes=[pltpu.VMEM((tm, D_MODEL), BF16)],
        compiler_params=_cparams(("parallel",), 32),
        name="outproj",
    )(x, oa, z, od, orr, z, ga, gret, w, gpost)


def _regroup_w_in(w):
    pad = jnp.zeros((w.shape[0], LANES - GLA_RANK), w.dtype)
    return jnp.concatenate([w[:, GLA_RANK:], w[:, :GLA_RANK], pad], axis=1).astype(BF16)


def _state_to_blockdiag_t(s):
    b, h, dk, dv = s.shape
    eye = jnp.eye(h, dtype=s.dtype)
    return jnp.einsum('bhdv,hg->bhvgd', s, eye).reshape(b, h * dv, h * dk)


def _blockdiag_t_to_state(st, h, dk, dv):
    b = st.shape[0]
    blocks = st.reshape(b, h, dv, h, dk)
    diag = jnp.stack([blocks[:, i, :, i, :] for i in range(h)], axis=1)
    return diag.swapaxes(-1, -2)


def _layer(x, l, depth, p, past_k, past_v, s_gla0, s_ret0, kv_prev):
    b, t, _ = x.shape
    n = b * t
    x = _ffn(x.reshape(n, D_MODEL), p['ffn1_norm_pre'][l], p['ffn1_w_up'][l], p['ffn1_w_down'][l],
             p['ffn1_norm_post'][l])
    z, k4, v4, kb, vb = _inproj(x, p['mix_norm_pre'][l], p['w_in'][l], p['w_gla_a2'][l], p['b_gla_a'][l],
                                l, depth, kv_prev)
    z3 = z.reshape(b, t, Z_W)
    kb3 = kb.reshape(b, t, W_DIFF)
    vb3 = vb.reshape(b, t, W_DIFF)

    o_a, st_gla = _gla(z3, s_gla0)

    out_scale = 1.0 - (0.8 - 0.6 * math.exp(-0.3 * l))
    if past_k is None:
        o_d = _attn_prompt(p['lam'][l], z3, kb3, vb3, p['diff_subln'][l], out_scale)
        past_len = 0
    else:
        o_d = _attn_sample(p['lam'][l], z3, past_k, past_v, kb3, vb3, p['diff_subln'][l], out_scale)
        past_len = past_k.shape[1]

    o_r, st_ret = _ret(z3, s_ret0, past_len)

    x = _outproj(x, o_a.reshape(n, W_GLA), z, o_d.reshape(n, W_DIFF), o_r.reshape(n, W_RET),
                 p['gla_norm'][l], p['ret_norm'][l], p['w_out'][l], p['mix_norm_post'][l])
    x = _ffn(x, p['ffn2_norm_pre'][l], p['ffn2_w_up'][l], p['ffn2_w_down'][l], p['ffn2_norm_post'][l])
    return x.reshape(b, t, D_MODEL), (k4, v4), st_gla, st_ret


def kernel(x_prompt, x_sample, cache_diff_k, cache_diff_v, state_gla, state_ret, ffn1_norm_pre, ffn1_w_up, ffn1_w_down, ffn1_norm_post, mix_norm_pre, w_in, w_gla_a2, b_gla_a, gla_norm, diff_lambda, diff_subln, ret_norm, w_out, mix_norm_post, ffn2_norm_pre, ffn2_w_up, ffn2_w_down, ffn2_norm_post):
    depth = w_in.shape[0]
    row = lambda a: a.reshape(depth, 1, -1)
    lam_p = diff_lambda.astype(F32)
    lam_init = jnp.asarray([0.8 - 0.6 * math.exp(-0.3 * l) for l in range(depth)], F32)
    lam = (jnp.exp(jnp.sum(lam_p[:, 0] * lam_p[:, 1], axis=-1))
           - jnp.exp(jnp.sum(lam_p[:, 2] * lam_p[:, 3], axis=-1)) + lam_init)
    p = {
        'ffn1_norm_pre': row(ffn1_norm_pre), 'ffn1_w_up': ffn1_w_up.astype(BF16),
        'ffn1_w_down': ffn1_w_down.astype(BF16), 'ffn1_norm_post': row(ffn1_norm_post),
        'mix_norm_pre': row(mix_norm_pre),
        'w_in': jnp.stack([_regroup_w_in(w_in[l]) for l in range(depth)]),
        'w_gla_a2': jnp.pad(w_gla_a2, ((0, 0), (0, LANES - GLA_RANK), (0, 0))).astype(BF16),
        'b_gla_a': row(b_gla_a), 'gla_norm': row(gla_norm), 'lam': lam.reshape(depth, 1),
        'diff_subln': row(diff_subln), 'ret_norm': row(ret_norm), 'w_out': w_out.astype(BF16),
        'mix_norm_post': row(mix_norm_post),
        'ffn2_norm_pre': row(ffn2_norm_pre), 'ffn2_w_up': ffn2_w_up.astype(BF16),
        'ffn2_w_down': ffn2_w_down.astype(BF16), 'ffn2_norm_post': row(ffn2_norm_post),
    }
    bp = x_prompt.shape[0]
    bs = x_sample.shape[0]
    zeros_gla = jnp.zeros((bp, W_GLA, H_A * DK_A), F32)
    zeros_ret = jnp.zeros((bp, W_RET, H_R * DK_R), F32)
    xp, xs = x_prompt, x_sample
    kv_p = kv_s = None
    st_p, st_s = [], []
    for l in range(depth):
        xp, kv_p, g_, r_ = _layer(xp, l, depth, p, None, None, zeros_gla, zeros_ret, kv_p)
        st_p.append((g_, r_))
        xs, kv_s, g_, r_ = _layer(xs, l, depth, p, cache_diff_k[l], cache_diff_v[l],
                                  _state_to_blockdiag_t(state_gla[l]), _state_to_blockdiag_t(state_ret[l]), kv_s)
        st_s.append((g_, r_))

    def states(sts):
        g = jnp.stack([_blockdiag_t_to_state(s[0], H_A, DK_A, DV_A) for s in sts])
        r = jnp.stack([_blockdiag_t_to_state(s[1], H_R, DK_R, DV_R) for s in sts])
        return g, r

    gp, rp = states(st_p)
    gs, rs = states(st_s)
    kv4 = lambda a, b: a.reshape(depth, b, -1, H_D, 2 * DK_D)
    return (xp, xs, kv4(kv_p[0], bp), kv4(kv_p[1], bp), gp, rp, kv4(kv_s[0], bs), kv4(kv_s[1], bs), gs, rs)
```

```python
import functools
import math

import jax
import jax.numpy as jnp
from jax import lax
from jax.experimental import pallas as pl
from jax.experimental.pallas import tpu as pltpu

F32 = jnp.float32
BF16 = jnp.bfloat16

D_MODEL = 1024
D_FF = 2816
EPS = 1e-6
NEG_INF = -1e30
CHUNK = 64

H_A, DK_A, DV_A = 4, 32, 64
GLA_RANK = 16
GLA_TAU = 16.0
H_D, DK_D, DV_D = 4, 64, 128
H_R, DK_R, DV_R = 4, 64, 64
ROPE_BASE = 10000.0

W_GLA = H_A * DV_A
W_DIFF = H_D * DV_D
W_RET = H_R * DV_R

LANES = 128
SUBLANES = 8

WP_DK = 1280
WP_DV = 1792
WP_RET = 2304
WP_ALR = 3328
WP_W = 3456
Z_GQ, Z_GK, Z_GV, Z_GR, Z_DQ = 0, 128, 256, 512, 768
Z_RQ, Z_RK, Z_RV, Z_RG = 1280, 1536, 1792, 2048
Z_LA = 2304
Z_W = 2432

TOKEN_TILE = 512
FF_COLS = 256
SCAN_TILE = 512
RET_CHUNK = 128
GLA_SUB = 16
ATTN_TILE = 256
ATTN_CTX_TILE = 512

_MIB = 1024 * 1024


def _cparams(sem, vmem_mib):
    return pltpu.CompilerParams(dimension_semantics=sem, vmem_limit_bytes=vmem_mib * _MIB)


def _rms(x, g):
    return x * lax.rsqrt(jnp.mean(x * x, axis=-1, keepdims=True) + EPS) * g


def _sigmoid(x):
    return 1.0 / (1.0 + jnp.exp(-x))


def _const_spec(shape):
    n = len(shape)
    return pl.BlockSpec(shape, lambda *_: (0,) * n, pipeline_mode=pl.Buffered(1))


def _ffn_math(x, gpre_ref, wup_ref, wdown_ref, gpost_ref, h_scr, act_scr):
    h_scr[...] = _rms(x, gpre_ref[...]).astype(BF16)
    for c in range(D_FF // FF_COLS):
        lo = c * FF_COLS
        gate = jnp.dot(h_scr[...], wup_ref[:, lo:lo + FF_COLS], preferred_element_type=F32)
        up = jnp.dot(h_scr[...], wup_ref[:, D_FF + lo:D_FF + lo + FF_COLS], preferred_element_type=F32)
        act_scr[:, lo:lo + FF_COLS] = (gate * _sigmoid(gate) * up).astype(BF16)
    y = jnp.dot(act_scr[...], wdown_ref[...], preferred_element_type=F32)
    return x + 0.5 * _rms(y, gpost_ref[...])


def _ffn_kernel(x_ref, gpre_ref, wup_ref, wdown_ref, gpost_ref, o_ref, h_scr, act_scr):
    o_ref[...] = _ffn_math(x_ref[...], gpre_ref, wup_ref, wdown_ref, gpost_ref, h_scr, act_scr)


def _ffn(x, gpre, wup, wdown, gpost):
    n = x.shape[0]
    tm = min(TOKEN_TILE, n)
    return pl.pallas_call(
        _ffn_kernel,
        grid=(n // tm,),
        in_specs=[
            pl.BlockSpec((tm, D_MODEL), lambda i: (i, 0)),
            _const_spec((1, D_MODEL)),
            _const_spec((D_MODEL, 2 * D_FF)),
            _const_spec((D_FF, D_MODEL)),
            _const_spec((1, D_MODEL)),
        ],
        out_specs=pl.BlockSpec((tm, D_MODEL), lambda i: (i, 0)),
        out_shape=jax.ShapeDtypeStruct((n, D_MODEL), F32),
        scratch_shapes=[pltpu.VMEM((tm, D_MODEL), BF16), pltpu.VMEM((tm, D_FF), BF16)],
        compiler_params=_cparams(("parallel",), 48),
        name="ffn",
    )(x, gpre, wup, wdown, gpost)


def _inproj_kernel(*refs, n_alias, slot):
    x_ref, g_ref, w_ref, wa2_ref, ba_ref = refs[:5]
    z_ref, k4_ref, v4_ref, kb_ref, vb_ref, h_scr = refs[5 + n_alias:]
    for other in range(k4_ref.shape[0]):
        if other != slot:
            k4_ref[other] = jnp.zeros(k4_ref.shape[1:], F32)
            v4_ref[other] = jnp.zeros(v4_ref.shape[1:], F32)
    h_scr[...] = _rms(x_ref[...], g_ref[...]).astype(BF16)
    for lo in range(0, WP_DK, 256):
        z_ref[:, lo:lo + 256] = jnp.dot(h_scr[...], w_ref[:, lo:lo + 256], preferred_element_type=F32)
    for lo in range(0, WP_ALR - WP_RET, 256):
        z_ref[:, Z_RQ + lo:Z_RQ + lo + 256] = jnp.dot(
            h_scr[...], w_ref[:, WP_RET + lo:WP_RET + lo + 256], preferred_element_type=F32)
    a_lr = jnp.dot(h_scr[...], w_ref[:, WP_ALR:WP_ALR + LANES], preferred_element_type=F32)
    pre = jnp.dot(a_lr.astype(BF16), wa2_ref[...], preferred_element_type=F32) + ba_ref[...]
    log_sig = jnp.minimum(pre, 0.0) - jnp.log(1.0 + jnp.exp(-jnp.abs(pre)))
    z_ref[:, Z_LA:Z_LA + LANES] = log_sig / GLA_TAU
    hw = 2 * DK_D
    for lo in range(0, W_DIFF, 2 * hw):
        k_pair = jnp.dot(h_scr[...], w_ref[:, WP_DK + lo:WP_DK + lo + 2 * hw], preferred_element_type=F32)
        v_pair = jnp.dot(h_scr[...], w_ref[:, WP_DV + lo:WP_DV + lo + 2 * hw], preferred_element_type=F32)
        kb_ref[:, lo:lo + 2 * hw] = k_pair.astype(BF16)
        vb_ref[:, lo:lo + 2 * hw] = v_pair.astype(BF16)
        for i in range(2):
            k4_ref[slot, :, lo // hw + i, :] = k_pair[:, i * hw:(i + 1) * hw]
            v4_ref[slot, :, lo // hw + i, :] = v_pair[:, i * hw:(i + 1) * hw]


def _inproj(x, g, w, wa2, ba, layer, depth, kv_prev):
    n = x.shape[0]
    tm = min(TOKEN_TILE, n)
    hw = 2 * DK_D
    n_alias = 0 if kv_prev is None else 2
    kv_shape = jax.ShapeDtypeStruct((depth, n, H_D, hw), F32)
    if kv_prev is None:
        kv_spec = pl.BlockSpec((depth, tm, H_D, hw), lambda i: (0, i, 0, 0))
        slot = layer
    else:
        kv_spec = pl.BlockSpec((1, tm, H_D, hw), lambda i: (layer, i, 0, 0))
        slot = 0
    return pl.pallas_call(
        functools.partial(_inproj_kernel, n_alias=n_alias, slot=slot),
        grid=(n // tm,),
        in_specs=[
            pl.BlockSpec((tm, D_MODEL), lambda i: (i, 0)),
            _const_spec((1, D_MODEL)),
            _const_spec((D_MODEL, WP_W)),
            _const_spec((LANES, LANES)),
            _const_spec((1, LANES)),
        ] + [pl.BlockSpec(memory_space=pl.ANY)] * n_alias,
        out_specs=[
            pl.BlockSpec((tm, Z_W), lambda i: (i, 0)),
            kv_spec, kv_spec,
            pl.BlockSpec((tm, W_DIFF), lambda i: (i, 0)),
            pl.BlockSpec((tm, W_DIFF), lambda i: (i, 0)),
        ],
        out_shape=[
            jax.ShapeDtypeStruct((n, Z_W), F32),
            kv_shape, kv_shape,
            jax.ShapeDtypeStruct((n, W_DIFF), BF16),
            jax.ShapeDtypeStruct((n, W_DIFF), BF16),
        ],
        input_output_aliases={} if kv_prev is None else {5: 1, 6: 2},
        scratch_shapes=[pltpu.VMEM((tm, D_MODEL), BF16)],
        compiler_params=_cparams(("parallel",), 56),
        name="inproj",
    )(x, g, w, wa2, ba, *(() if kv_prev is None else kv_prev))


def _gla_kernel(q_ref, k_ref, v_ref, la_ref, s0_ref, o_ref, st_ref, st_scr, *, n_chunks):
    c = CHUNK

    @pl.when(pl.program_id(1) == 0)
    def _():
        st_scr[...] = s0_ref[0]

    r_i = lax.broadcasted_iota(jnp.int32, (c, c), 0)
    c_i = lax.broadcasted_iota(jnp.int32, (c, c), 1)
    tri = (r_i >= c_i).astype(BF16)
    head_k = lax.broadcasted_iota(jnp.int32, (H_A * DK_A, W_GLA), 0) // DK_A
    head_v = lax.broadcasted_iota(jnp.int32, (H_A * DK_A, W_GLA), 1) // DV_A
    sum_bcast = (head_k == head_v).astype(BF16)
    st_hv = lax.broadcasted_iota(jnp.int32, (W_GLA, H_A * DK_A), 0) // DV_A
    st_hk = lax.broadcasted_iota(jnp.int32, (W_GLA, H_A * DK_A), 1) // DK_A
    st_mask = st_hv == st_hk
    sub = lax.broadcasted_iota(jnp.int32, (SUBLANES, H_A * DK_A), 0)
    n_groups = c // SUBLANES
    sub_groups = GLA_SUB // SUBLANES
    head_q = lax.broadcasted_iota(jnp.int32, (GLA_SUB, H_A * DK_A), 1) // DK_A
    head_o = lax.broadcasted_iota(jnp.int32, (GLA_SUB, W_GLA), 1) // DV_A

    n_sub = c // GLA_SUB

    def front(ci):
        rows = slice(ci * c, (ci + 1) * c)
        q = q_ref[0, rows, :] * (DK_A ** -0.5)
        k = k_ref[0, rows, :]
        la = la_ref[0, rows, :]

        la1 = la.astype(BF16)
        rem = la - la1.astype(F32)
        la2 = rem.astype(BF16)
        la3 = (rem - la2.astype(F32)).astype(BF16)
        b3 = jnp.dot(tri, jnp.concatenate([la1, la2, la3], axis=1), preferred_element_type=F32)
        b = b3[:, :LANES] + b3[:, LANES:2 * LANES] + b3[:, 2 * LANES:]

        pieces = []
        for s in range(c):
            blk_end = (s // GLA_SUB + 1) * GLA_SUB
            r0 = (s // SUBLANES) * SUBLANES
            n_r = blk_end - r0
            e = jnp.exp(jnp.minimum(b[r0:blk_end, :] - b[s:s + 1, :], 0.0))
            p = q[r0:blk_end, :] * e * k[s:s + 1, :]
            if s % SUBLANES:
                first = jnp.where(sub >= (s % SUBLANES), p[:SUBLANES], 0.0)
                p = first if n_r == SUBLANES else jnp.concatenate([first, p[SUBLANES:]], axis=0)
            pieces.append(p)
        p_all = jnp.concatenate(pieces, axis=0).astype(BF16)
        w_all = jnp.dot(p_all, sum_bcast, preferred_element_type=F32)
        a_blocks = []
        for i in range(1, n_sub):
            lo = i * GLA_SUB
            ref = b[lo - 1:lo, :]
            q_i = q[lo:lo + GLA_SUB, :] * jnp.exp(b[lo:lo + GLA_SUB, :] - ref)
            k_i = (k[:lo, :] * jnp.exp(ref - b[:lo, :])).astype(BF16)
            q_heads = jnp.concatenate(
                [jnp.where(head_q == hh, q_i, 0.0) for hh in range(H_A)], axis=0).astype(BF16)
            a_blocks.append(lax.dot_general(q_heads, k_i, (((1,), (1,)), ((), ())),
                                            preferred_element_type=F32).astype(BF16))
        b_last = b[c - 1:c, :]
        return dict(w_all=w_all, a_blocks=a_blocks, q_dec=(q * jnp.exp(b)).astype(BF16),
                    k_dec=(k * jnp.exp(b_last - b)).astype(BF16), decay=jnp.exp(b_last))

    def middle(ci, f):
        v_bf = v_ref[0, ci * c:(ci + 1) * c, :].astype(BF16)
        f['r_blocks'] = [jnp.dot(a, v_bf[:(i + 1) * GLA_SUB, :], preferred_element_type=F32)
                         for i, a in enumerate(f.pop('a_blocks'))]
        f['upd'] = lax.dot_general(v_bf, f.pop('k_dec'), (((0,), (0,)), ((), ())), preferred_element_type=F32)
        return f

    def back(ci, f, st):
        v = v_ref[0, ci * c:(ci + 1) * c, :]
        w_all = f['w_all']
        acc = [None] * n_groups
        off = 0
        for s in range(c):
            v_s = v[s:s + 1, :]
            for g in range(s // SUBLANES, (s // GLA_SUB + 1) * GLA_SUB // SUBLANES):
                t = w_all[off:off + SUBLANES, :] * v_s
                acc[g] = t if acc[g] is None else acc[g] + t
                off += SUBLANES
        o_blocks = [jnp.concatenate(acc[i * sub_groups:(i + 1) * sub_groups], axis=0) for i in range(n_sub)]
        for i, r_i in enumerate(f['r_blocks']):
            for hh in range(H_A):
                o_blocks[i + 1] = o_blocks[i + 1] + jnp.where(
                    head_o == hh, r_i[hh * GLA_SUB:(hh + 1) * GLA_SUB], 0.0)
        o_inter = lax.dot_general(f['q_dec'], st.astype(BF16), (((1,), (1,)), ((), ())),
                                  preferred_element_type=F32)
        st_new = st * f['decay'] + jnp.where(st_mask, f['upd'], 0.0)
        return st_new, jnp.concatenate(o_blocks, axis=0) + o_inter

    st = st_scr[...]
    outs = []
    stage = {}
    for t in range(n_chunks + 2):
        if t < n_chunks:
            stage[t] = front(t)
        if 0 <= t - 1 < n_chunks:
            stage[t - 1] = middle(t - 1, stage[t - 1])
        if 0 <= t - 2 < n_chunks:
            st, o = back(t - 2, stage.pop(t - 2), st)
            outs.append(o)
    for ci in range(n_chunks):
        o_ref[0, ci * c:(ci + 1) * c, :] = outs[ci]
    st_scr[...] = st
    st_ref[0] = st


def _gla(z, s0t):
    b, t, _ = z.shape
    tb = min(SCAN_TILE, t)
    kw = H_A * DK_A
    return pl.pallas_call(
        functools.partial(_gla_kernel, n_chunks=tb // CHUNK),
        grid=(b, t // tb),
        in_specs=[
            pl.BlockSpec((1, tb, kw), lambda i, j: (i, j, Z_GQ // kw)),
            pl.BlockSpec((1, tb, kw), lambda i, j: (i, j, Z_GK // kw)),
            pl.BlockSpec((1, tb, W_GLA), lambda i, j: (i, j, Z_GV // W_GLA)),
            pl.BlockSpec((1, tb, kw), lambda i, j: (i, j, Z_LA // kw)),
            pl.BlockSpec((1, W_GLA, kw), lambda i, j: (i, 0, 0)),
        ],
        out_specs=[
            pl.BlockSpec((1, tb, W_GLA), lambda i, j: (i, j, 0)),
            pl.BlockSpec((1, W_GLA, kw), lambda i, j: (i, 0, 0)),
        ],
        out_shape=[
            jax.ShapeDtypeStruct((b, t, W_GLA), F32),
            jax.ShapeDtypeStruct((b, W_GLA, kw), F32),
        ],
        scratch_shapes=[pltpu.VMEM((W_GLA, kw), F32)],
        compiler_params=_cparams(("parallel", "arbitrary"), 32),
        name="gla",
    )(z, z, z, z, s0t)


def _swap_halves(x):
    lane = lax.broadcasted_iota(jnp.int32, (x.shape[0], LANES), 1)
    first_half = (lane % DK_R) < (DK_R // 2)
    cols = []
    for lo in range(0, x.shape[1], LANES):
        blk = x[:, lo:lo + LANES]
        up = pltpu.roll(blk, LANES - DK_R // 2, axis=1)
        down = pltpu.roll(blk, DK_R // 2, axis=1)
        cols.append(jnp.where(first_half, up, down))
    return jnp.concatenate(cols, axis=1)


def _ret_kernel(q_ref, k_ref, v_ref, cos_ref, sin_ref, dec_ref, qdec_ref, kdec_ref, cdec_ref, s0_ref,
                o_ref, st_ref, st_scr, *, n_chunks, c):
    @pl.when(pl.program_id(1) == 0)
    def _():
        st_scr[...] = s0_ref[0]

    head_l = lax.broadcasted_iota(jnp.int32, (c, W_RET), 1) // DK_R
    st_mask = (lax.broadcasted_iota(jnp.int32, (W_RET, W_RET), 0) // DV_R
               == lax.broadcasted_iota(jnp.int32, (W_RET, W_RET), 1) // DK_R)

    def front(ci):
        rows = slice(ci * c, (ci + 1) * c)
        cos = cos_ref[rows, :]
        sin = sin_ref[rows, :]
        q = q_ref[0, rows, :]
        k = k_ref[0, rows, :]
        q = q * cos + _swap_halves(q) * sin
        k = (k * cos + _swap_halves(k) * sin) * (DK_R ** -0.5)
        q_stack = jnp.concatenate(
            [jnp.where(head_l == h, q, 0.0) for h in range(H_R)], axis=0).astype(BF16)
        s = lax.dot_general(q_stack, k.astype(BF16), (((1,), (1,)), ((), ())),
                            preferred_element_type=F32)
        return dict(s=s, q_dec=(q * qdec_ref[...]).astype(BF16), k_dec=(k * kdec_ref[...]).astype(BF16))

    def middle(ci, f):
        v = v_ref[0, ci * c:(ci + 1) * c, :].astype(BF16)
        s = (f.pop('s') * dec_ref[...]).astype(BF16)
        f['r'] = jnp.dot(s, v, preferred_element_type=F32)
        f['upd'] = lax.dot_general(v, f.pop('k_dec'), (((0,), (0,)), ((), ())), preferred_element_type=F32)
        return f

    def back(f, st):
        r = f['r']
        o = jnp.where(head_l == 0, r[:c], 0.0)
        for h in range(1, H_R):
            o = o + jnp.where(head_l == h, r[h * c:(h + 1) * c], 0.0)
        o = o + lax.dot_general(f['q_dec'], st.astype(BF16), (((1,), (1,)), ((), ())),
                                preferred_element_type=F32)
        return st * cdec_ref[...] + jnp.where(st_mask, f['upd'], 0.0), o

    st = st_scr[...]
    outs = []
    stage = {}
    for t in range(n_chunks + 2):
        if t < n_chunks:
            stage[t] = front(t)
        if 0 <= t - 1 < n_chunks:
            stage[t - 1] = middle(t - 1, stage[t - 1])
        if 0 <= t - 2 < n_chunks:
            st, o = back(stage.pop(t - 2), st)
            outs.append(o)
    for ci in range(n_chunks):
        o_ref[0, ci * c:(ci + 1) * c, :] = outs[ci]
    st_scr[...] = st
    st_ref[0] = st


def _ret_tables(c):
    lg = jnp.log(1.0 - 2.0 ** (-5.0 - jnp.arange(H_R, dtype=F32)))
    idx = jnp.arange(c, dtype=F32)
    tri = idx[:, None] >= idx[None, :]
    dec = jnp.exp(jnp.where(tri[None], (idx[:, None] - idx[None, :])[None] * lg[:, None, None], -jnp.inf))
    q_dec = jnp.exp((idx[None, :] + 1.0) * lg[:, None])
    k_dec = jnp.exp((c - 1.0 - idx[None, :]) * lg[:, None])
    c_dec = jnp.exp(c * lg)
    return (dec.reshape(H_R * c, c),
            jnp.repeat(q_dec.T, DK_R, axis=1),
            jnp.repeat(k_dec.T, DK_R, axis=1),
            jnp.repeat(c_dec, DK_R)[None, :])


def _rope_tables(pos):
    half = DK_R // 2
    inv = ROPE_BASE ** (-jnp.arange(half, dtype=F32) / half)
    ang = pos.astype(F32)[:, None] * inv[None, :]
    cos = jnp.cos(ang)
    sin = jnp.sin(ang)
    cos_t = jnp.tile(jnp.concatenate([cos, cos], axis=1), (1, H_R))
    sin_t = jnp.tile(jnp.concatenate([-sin, sin], axis=1), (1, H_R))
    return cos_t, sin_t


def _ret(z, s0t, past_len):
    b, t, _ = z.shape
    tb = min(SCAN_TILE, t)
    c = min(RET_CHUNK, t)
    cos_t, sin_t = _rope_tables(past_len + jnp.arange(t))
    dec, q_dec, k_dec, c_dec = _ret_tables(c)
    return pl.pallas_call(
        functools.partial(_ret_kernel, n_chunks=tb // c, c=c),
        grid=(b, t // tb),
        in_specs=[
            pl.BlockSpec((1, tb, W_RET), lambda i, j: (i, j, Z_RQ // W_RET)),
            pl.BlockSpec((1, tb, W_RET), lambda i, j: (i, j, Z_RK // W_RET)),
            pl.BlockSpec((1, tb, W_RET), lambda i, j: (i, j, Z_RV // W_RET)),
            pl.BlockSpec((tb, W_RET), lambda i, j: (j, 0)),
            pl.BlockSpec((tb, W_RET), lambda i, j: (j, 0)),
            _const_spec((H_R * c, c)),
            _const_spec((c, W_RET)),
            _const_spec((c, W_RET)),
            _const_spec((1, W_RET)),
            pl.BlockSpec((1, W_RET, W_RET), lambda i, j: (i, 0, 0)),
        ],
        out_specs=[
            pl.BlockSpec((1, tb, W_RET), lambda i, j: (i, j, 0)),
            pl.BlockSpec((1, W_RET, W_RET), lambda i, j: (i, 0, 0)),
        ],
        out_shape=[
            jax.ShapeDtypeStruct((b, t, W_RET), F32),
            jax.ShapeDtypeStruct((b, W_RET, W_RET), F32),
        ],
        scratch_shapes=[pltpu.VMEM((W_RET, W_RET), F32)],
        compiler_params=_cparams(("parallel", "arbitrary"), 32),
        name="ret",
    )(z, z, z, cos_t, sin_t, dec, q_dec, k_dec, c_dec, s0t)


V_ROWS = DV_D + 16


LOG2E = 1.4426950408889634


def _stack_maps(q):
    lane = lax.broadcasted_iota(jnp.int32, q.shape, 1)
    q = q * (DK_D ** -0.5 * LOG2E)
    return jnp.concatenate(
        [jnp.where(lane < DK_D, q, 0.0), jnp.where(lane >= DK_D, q, 0.0)], axis=0).astype(BF16)


def _scores_t(k_tile, q_stack):
    return lax.dot_general(k_tile, q_stack, (((1,), (1,)), ((), ())), preferred_element_type=F32)


def _softmax_t(s, rel, shift, m_old):
    s = s + rel
    m_new = jnp.maximum(m_old, jnp.max(s, axis=0, keepdims=True) + shift)
    p = jnp.exp2(s - (m_new - shift)).astype(BF16)
    return m_new, p, jnp.exp2(m_old - m_new)


def _accumulate_t(acc, alpha, vt_tile, p):
    return alpha * acc + jnp.dot(vt_tile, p, preferred_element_type=F32)


def _attn_finish(acc, lam, g, out_scale, tq):
    o_t = acc[0:DV_D] / acc[DV_D:DV_D + 1]
    o = o_t[:, :tq] - lam * o_t[:, tq:]
    y_t = o * lax.rsqrt(jnp.mean(o * o, axis=0, keepdims=True) + EPS)
    return y_t.T * g * out_scale


def _with_ones(v_t):
    return jnp.concatenate([v_t, jnp.ones((V_ROWS - DV_D, v_t.shape[1]), BF16)], axis=0)


def _attn_prompt_kernel(lam_ref, slope_ref, q_ref, kb_ref, vb_ref, relc_ref, reld_ref, g_ref, o_ref,
                        vt_scr, s_a, s_b, p_a, p_b, acc_scr, *, tq, n_tiles, out_scale):
    h = pl.program_id(1)
    for i in range(n_tiles):
        v_t = vb_ref[0, i * tq:(i + 1) * tq, :].astype(F32).T.astype(BF16)
        vt_scr[i] = _with_ones(v_t)
    slope2 = slope_ref[h] * LOG2E
    s_bufs = (s_a, s_b)
    p_bufs = (p_a, p_b)
    col_groups = range(0, 2 * tq, LANES)

    def softmax_all(s_ref, rel_ref, shift, m_old):
        return [_softmax_t(s_ref[:, lo:lo + LANES], rel_ref[0, :, lo:lo + LANES], shift, m_old[i])
                for i, lo in enumerate(col_groups)]

    for qi in range(n_tiles):
        q_stack = _stack_maps(q_ref[0, qi * tq:(qi + 1) * tq, :])

        def scores(j):
            return _scores_t(kb_ref[0, j * tq:(j + 1) * tq, :], q_stack)

        s_bufs[0][...] = scores(0)
        m = [jnp.full((1, LANES), NEG_INF, F32) for _ in col_groups]
        alpha_prev = None
        for j in range(qi):
            s_in, s_out = s_bufs[j % 2], s_bufs[1 - j % 2]
            p_in, p_out = p_bufs[1 - j % 2], p_bufs[j % 2]
            stats = softmax_all(s_in, relc_ref, -slope2 * float((qi - j) * tq), m)
            s_next = scores(j + 1)
            if j == 1:
                acc_new = jnp.dot(vt_scr[0], p_in[...], preferred_element_type=F32)
            elif j > 1:
                acc_new = _accumulate_t(acc_scr[...], alpha_prev, vt_scr[j - 1], p_in[...])
            for (m_new, p, alpha), lo in zip(stats, col_groups):
                p_out[:, lo:lo + LANES] = p
            m = [st[0] for st in stats]
            alpha_prev = jnp.concatenate([st[2] for st in stats], axis=1)
            s_out[...] = s_next
            if j >= 1:
                acc_scr[...] = acc_new
        stats = softmax_all(s_bufs[qi % 2], reld_ref, 0.0, m)
        p = jnp.concatenate([st[1] for st in stats], axis=1)
        alpha = jnp.concatenate([st[2] for st in stats], axis=1)
        if qi == 0:
            acc = jnp.dot(vt_scr[0], p, preferred_element_type=F32)
        else:
            if qi == 1:
                acc = jnp.dot(vt_scr[0], p_bufs[1 - qi % 2][...], preferred_element_type=F32)
            else:
                acc = _accumulate_t(acc_scr[...], alpha_prev, vt_scr[qi - 1], p_bufs[1 - qi % 2][...])
            acc = _accumulate_t(acc, alpha, vt_scr[qi], p)
        o_ref[0, qi * tq:(qi + 1) * tq, :] = _attn_finish(acc, lam_ref[0], g_ref[...], out_scale, tq)


def _attn_tables(tk, tq):
    slopes = jnp.asarray([2.0 ** (-8.0 * (i + 1) / H_D) for i in range(H_D)], F32)
    slopes2 = slopes * LOG2E
    key = lax.broadcasted_iota(jnp.int32, (tk, 2 * tq), 0)
    qry = lax.broadcasted_iota(jnp.int32, (tk, 2 * tq), 1) % tq
    rel_ctx = slopes2[:, None, None] * (key - qry).astype(F32)[None]
    key_d = lax.broadcasted_iota(jnp.int32, (tq, 2 * tq), 0)
    qry_d = lax.broadcasted_iota(jnp.int32, (tq, 2 * tq), 1) % tq
    visible = (key_d // CHUNK) <= (qry_d // CHUNK)
    rel_diag = jnp.where(visible[None], -slopes2[:, None, None] * jnp.abs(qry_d - key_d).astype(F32)[None],
                         NEG_INF)
    return slopes, rel_ctx, rel_diag


def _attn_prompt(lam, z3, kb, vb, g, out_scale):
    b, t, _ = z3.shape
    tq = min(ATTN_TILE, t)
    n_tiles = t // tq
    hw = 2 * DK_D
    slopes, rel_ctx, rel_diag = _attn_tables(tq, tq)
    smem = pl.BlockSpec(memory_space=pltpu.SMEM)
    return pl.pallas_call(
        functools.partial(_attn_prompt_kernel, tq=tq, n_tiles=n_tiles, out_scale=out_scale),
        grid=(b, H_D),
        in_specs=[
            smem, smem,
            pl.BlockSpec((1, t, hw), lambda i, h: (i, 0, Z_DQ // hw + h)),
            pl.BlockSpec((1, t, hw), lambda i, h: (i, 0, h)),
            pl.BlockSpec((1, t, DV_D), lambda i, h: (i, 0, h)),
            pl.BlockSpec((1, tq, 2 * tq), lambda i, h: (h, 0, 0)),
            pl.BlockSpec((1, tq, 2 * tq), lambda i, h: (h, 0, 0)),
            pl.BlockSpec((1, DV_D), lambda i, h: (0, 0)),
        ],
        out_specs=pl.BlockSpec((1, t, DV_D), lambda i, h: (i, 0, h)),
        out_shape=jax.ShapeDtypeStruct((b, t, W_DIFF), F32),
        scratch_shapes=[
            pltpu.VMEM((n_tiles, V_ROWS, tq), BF16),
            pltpu.VMEM((tq, 2 * tq), F32), pltpu.VMEM((tq, 2 * tq), F32),
            pltpu.VMEM((tq, 2 * tq), BF16), pltpu.VMEM((tq, 2 * tq), BF16),
            pltpu.VMEM((V_ROWS, 2 * tq), F32),
        ],
        compiler_params=_cparams(("parallel", "parallel"), 32),
        name="attn_prompt",
    )(lam, slopes, z3, kb, vb, rel_ctx, rel_diag, g)


def _attn_sample_kernel(lam_ref, slope_ref, q01_ref, q23_ref, kc_ref, vc_ref, kb_ref, vb_ref, relc_ref, reld_ref,
                        g_ref, o_ref, *, tq, tk, n_ctx, q_off, out_scale):
    hw = 2 * DK_D
    for h in range(H_D):
        slope2 = slope_ref[h] * LOG2E
        q_ref = q01_ref if h < 2 else q23_ref
        q_stack = _stack_maps(q_ref[0, :, (h % 2) * hw:(h % 2 + 1) * hw])
        m = jnp.full((1, 2 * tq), NEG_INF, F32)
        acc = jnp.zeros((V_ROWS, 2 * tq), F32)
        for j in range(n_ctx):
            k_tile = kc_ref[j * tk:(j + 1) * tk, h, :].astype(BF16)
            v_t = vc_ref[j * tk:(j + 1) * tk, h, :].T.astype(BF16)
            shift = -slope2 * float(q_off - j * tk)
            m, p, alpha = _softmax_t(_scores_t(k_tile, q_stack), relc_ref[h], shift, m)
            acc = _accumulate_t(acc, alpha, _with_ones(v_t), p)
        v_t = vb_ref[0, :, h * hw:(h + 1) * hw].astype(F32).T.astype(BF16)
        s = _scores_t(kb_ref[0, :, h * hw:(h + 1) * hw], q_stack)
        _, p, alpha = _softmax_t(s, reld_ref[h], 0.0, m)
        acc = _accumulate_t(acc, alpha, _with_ones(v_t), p)
        o_ref[0, :, h * hw:(h + 1) * hw] = _attn_finish(acc, lam_ref[0], g_ref[...], out_scale, tq)


def _attn_sample(lam, z3, cache_k, cache_v, layer, kb, vb, g, out_scale):
    b, t, _ = z3.shape
    t_past = cache_k.shape[2]
    tk = min(ATTN_CTX_TILE, t_past)
    slopes, rel_ctx, rel_diag = _attn_tables(tk, t)
    smem = pl.BlockSpec(memory_space=pltpu.SMEM)
    return pl.pallas_call(
        functools.partial(_attn_sample_kernel, tq=t, tk=tk, n_ctx=t_past // tk, q_off=t_past,
                          out_scale=out_scale),
        grid=(b,),
        in_specs=[
            smem, smem,
            pl.BlockSpec((1, t, W_DIFF // 2), lambda i: (i, 0, Z_DQ // (W_DIFF // 2))),
            pl.BlockSpec((1, t, W_DIFF // 2), lambda i: (i, 0, Z_DQ // (W_DIFF // 2) + 1)),
            pl.BlockSpec((None, None, t_past, H_D, 2 * DK_D), lambda i: (layer, i, 0, 0, 0)),
            pl.BlockSpec((None, None, t_past, H_D, DV_D), lambda i: (layer, i, 0, 0, 0)),
            pl.BlockSpec((1, t, W_DIFF), lambda i: (i, 0, 0)),
            pl.BlockSpec((1, t, W_DIFF), lambda i: (i, 0, 0)),
            _const_spec((H_D, tk, 2 * t)),
            _const_spec((H_D, t, 2 * t)),
            _const_spec((1, DV_D)),
        ],
        out_specs=pl.BlockSpec((1, t, W_DIFF), lambda i: (i, 0, 0)),
        out_shape=jax.ShapeDtypeStruct((b, t, W_DIFF), F32),
        compiler_params=_cparams(("parallel",), 48),
        name="attn_sample",
    )(lam, slopes, z3, z3, cache_k, cache_v, kb, vb, rel_ctx, rel_diag, g)


def _seg_mean(x, avg):
    hi = x.astype(BF16)
    lo = (x - hi.astype(F32)).astype(BF16)
    return jnp.dot(jnp.concatenate([hi, lo], axis=1), avg, preferred_element_type=F32)


def _outproj_ffn_kernel(x_ref, oa_ref, gr_ref, od_ref, or_ref, rg_ref, ga_ref, gret_ref, w_ref, gmix_ref,
                        gpre_ref, wup_ref, wdown_ref, gpost_ref, o_ref, h_scr, act_scr):
    lane_r = lax.broadcasted_iota(jnp.int32, (2 * W_GLA, W_GLA), 0) % W_GLA // DV_A
    lane_c = lax.broadcasted_iota(jnp.int32, (2 * W_GLA, W_GLA), 1) // DV_A
    avg = jnp.where(lane_r == lane_c, 1.0 / DV_A, 0.0).astype(BF16)

    oa = oa_ref[...]
    gr = gr_ref[...]
    oa_n = oa * lax.rsqrt(_seg_mean(oa * oa, avg) + EPS) * ga_ref[...]
    h_scr[:, 0:W_GLA] = (oa_n * (gr * _sigmoid(gr))).astype(BF16)

    h_scr[:, W_GLA:W_GLA + W_DIFF] = od_ref[...].astype(BF16)

    orr = or_ref[...]
    rg = rg_ref[...]
    cen = orr - _seg_mean(orr, avg)
    or_n = cen * lax.rsqrt(_seg_mean(cen * cen, avg) + EPS) * gret_ref[...]
    h_scr[:, W_GLA + W_DIFF:] = (or_n * (rg * _sigmoid(rg))).astype(BF16)

    m = jnp.dot(h_scr[...], w_ref[...], preferred_element_type=F32)
    x = x_ref[...] + _rms(m, gmix_ref[...])
    o_ref[...] = _ffn_math(x, gpre_ref, wup_ref, wdown_ref, gpost_ref, h_scr, act_scr)


def _outproj_ffn(x, oa, z, od, orr, ga, gret, w, gmix, gpre, wup, wdown, gpost):
    n = x.shape[0]
    tm = min(TOKEN_TILE, n)
    return pl.pallas_call(
        _outproj_ffn_kernel,
        grid=(n // tm,),
        in_specs=[
            pl.BlockSpec((tm, D_MODEL), lambda i: (i, 0)),
            pl.BlockSpec((tm, W_GLA), lambda i: (i, 0)),
            pl.BlockSpec((tm, W_GLA), lambda i: (i, Z_GR // W_GLA)),
            pl.BlockSpec((tm, W_DIFF), lambda i: (i, 0)),
            pl.BlockSpec((tm, W_RET), lambda i: (i, 0)),
            pl.BlockSpec((tm, W_RET), lambda i: (i, Z_RG // W_RET)),
            _const_spec((1, W_GLA)),
            _const_spec((1, W_RET)),
            _const_spec((D_MODEL, D_MODEL)),
            _const_spec((1, D_MODEL)),
            _const_spec((1, D_MODEL)),
            _const_spec((D_MODEL, 2 * D_FF)),
            _const_spec((D_FF, D_MODEL)),
            _const_spec((1, D_MODEL)),
        ],
        out_specs=pl.BlockSpec((tm, D_MODEL), lambda i: (i, 0)),
        out_shape=jax.ShapeDtypeStruct((n, D_MODEL), F32),
        scratch_shapes=[pltpu.VMEM((tm, D_MODEL), BF16), pltpu.VMEM((tm, D_FF), BF16)],
        compiler_params=_cparams(("parallel",), 52),
        name="outproj_ffn",
    )(x, oa, z, od, orr, z, ga, gret, w, gmix, gpre, wup, wdown, gpost)


def _regroup_w_in(w):
    pad = jnp.zeros((w.shape[0], LANES - GLA_RANK), w.dtype)
    return jnp.concatenate([w[:, GLA_RANK:], w[:, :GLA_RANK], pad], axis=1).astype(BF16)


def _state_to_blockdiag_t(s):
    b, h, dk, dv = s.shape
    eye = jnp.eye(h, dtype=s.dtype)
    return jnp.einsum('bhdv,hg->bhvgd', s, eye).reshape(b, h * dv, h * dk)


def _blockdiag_t_to_state(st, h, dk, dv):
    b = st.shape[0]
    blocks = st.reshape(b, h, dv, h, dk)
    diag = jnp.stack([blocks[:, i, :, i, :] for i in range(h)], axis=1)
    return diag.swapaxes(-1, -2)


def _layer(x, l, depth, p, past_k, past_v, s_gla0, s_ret0, kv_prev):
    b, t, _ = x.shape
    n = b * t
    x = _ffn(x.reshape(n, D_MODEL), p['ffn1_norm_pre'][l], p['ffn1_w_up'][l], p['ffn1_w_down'][l],
             p['ffn1_norm_post'][l])
    z, k4, v4, kb, vb = _inproj(x, p['mix_norm_pre'][l], p['w_in'][l], p['w_gla_a2'][l], p['b_gla_a'][l],
                                l, depth, kv_prev)
    z3 = z.reshape(b, t, Z_W)
    kb3 = kb.reshape(b, t, W_DIFF)
    vb3 = vb.reshape(b, t, W_DIFF)

    o_a, st_gla = _gla(z3, s_gla0)

    out_scale = 1.0 - (0.8 - 0.6 * math.exp(-0.3 * l))
    if past_k is None:
        o_d = _attn_prompt(p['lam'][l], z3, kb3, vb3, p['diff_subln'][l], out_scale)
        past_len = 0
    else:
        o_d = _attn_sample(p['lam'][l], z3, past_k, past_v, l, kb3, vb3, p['diff_subln'][l], out_scale)
        past_len = past_k.shape[2]

    o_r, st_ret = _ret(z3, s_ret0, past_len)

    x = _outproj_ffn(x, o_a.reshape(n, W_GLA), z, o_d.reshape(n, W_DIFF), o_r.reshape(n, W_RET),
                     p['gla_norm'][l], p['ret_norm'][l], p['w_out'][l], p['mix_norm_post'][l],
                     p['ffn2_norm_pre'][l], p['ffn2_w_up'][l], p['ffn2_w_down'][l], p['ffn2_norm_post'][l])
    return x.reshape(b, t, D_MODEL), (k4, v4), st_gla, st_ret


def kernel(x_prompt, x_sample, cache_diff_k, cache_diff_v, state_gla, state_ret, ffn1_norm_pre, ffn1_w_up, ffn1_w_down, ffn1_norm_post, mix_norm_pre, w_in, w_gla_a2, b_gla_a, gla_norm, diff_lambda, diff_subln, ret_norm, w_out, mix_norm_post, ffn2_norm_pre, ffn2_w_up, ffn2_w_down, ffn2_norm_post):
    depth = w_in.shape[0]
    row = lambda a: a.reshape(depth, 1, -1)
    lam_p = diff_lambda.astype(F32)
    lam_init = jnp.asarray([0.8 - 0.6 * math.exp(-0.3 * l) for l in range(depth)], F32)
    lam = (jnp.exp(jnp.sum(lam_p[:, 0] * lam_p[:, 1], axis=-1))
           - jnp.exp(jnp.sum(lam_p[:, 2] * lam_p[:, 3], axis=-1)) + lam_init)
    p = {
        'ffn1_norm_pre': row(ffn1_norm_pre), 'ffn1_w_up': ffn1_w_up.astype(BF16),
        'ffn1_w_down': ffn1_w_down.astype(BF16), 'ffn1_norm_post': row(ffn1_norm_post),
        'mix_norm_pre': row(mix_norm_pre),
        'w_in': jnp.stack([_regroup_w_in(w_in[l]) for l in range(depth)]),
        'w_gla_a2': jnp.pad(w_gla_a2, ((0, 0), (0, LANES - GLA_RANK), (0, 0))).astype(BF16),
        'b_gla_a': row(b_gla_a), 'gla_norm': row(gla_norm), 'lam': lam.reshape(depth, 1),
        'diff_subln': row(diff_subln), 'ret_norm': row(ret_norm), 'w_out': w_out.astype(BF16),
        'mix_norm_post': row(mix_norm_post),
        'ffn2_norm_pre': row(ffn2_norm_pre), 'ffn2_w_up': ffn2_w_up.astype(BF16),
        'ffn2_w_down': ffn2_w_down.astype(BF16), 'ffn2_norm_post': row(ffn2_norm_post),
    }
    bp = x_prompt.shape[0]
    bs = x_sample.shape[0]
    zeros_gla = jnp.zeros((bp, W_GLA, H_A * DK_A), F32)
    zeros_ret = jnp.zeros((bp, W_RET, H_R * DK_R), F32)
    xp, xs = x_prompt, x_sample
    kv_p = kv_s = None
    st_p, st_s = [], []
    for l in range(depth):
        xp, kv_p, g_, r_ = _layer(xp, l, depth, p, None, None, zeros_gla, zeros_ret, kv_p)
        st_p.append((g_, r_))
        xs, kv_s, g_, r_ = _layer(xs, l, depth, p, cache_diff_k, cache_diff_v,
                                  _state_to_blockdiag_t(state_gla[l]), _state_to_blockdiag_t(state_ret[l]), kv_s)
        st_s.append((g_, r_))

    def states(sts):
        g = jnp.stack([_blockdiag_t_to_state(s[0], H_A, DK_A, DV_A) for s in sts])
        r = jnp.stack([_blockdiag_t_to_state(s[1], H_R, DK_R, DV_R) for s in sts])
        return g, r

    gp, rp = states(st_p)
    gs, rs = states(st_s)
    kv4 = lambda a, b: a.reshape(depth, b, -1, H_D, 2 * DK_D)
    return (xp, xs, kv4(kv_p[0], bp), kv4(kv_p[1], bp), gp, rp, kv4(kv_s[0], bs), kv4(kv_s[1], bs), gs, rs)
```

```python
import functools
import math

import jax
import jax.numpy as jnp
from jax import lax
from jax.experimental import pallas as pl
from jax.experimental.pallas import tpu as pltpu

F32 = jnp.float32
BF16 = jnp.bfloat16

D_MODEL = 1024
D_FF = 2816
EPS = 1e-6
NEG_INF = -1e30
CHUNK = 64

H_A, DK_A, DV_A = 4, 32, 64
GLA_RANK = 16
GLA_TAU = 16.0
H_D, DK_D, DV_D = 4, 64, 128
H_R, DK_R, DV_R = 4, 64, 64
ROPE_BASE = 10000.0

W_GLA = H_A * DV_A
W_DIFF = H_D * DV_D
W_RET = H_R * DV_R

LANES = 128
SUBLANES = 8

WP_DK = 1280
WP_DV = 1792
WP_RET = 2304
WP_ALR = 3328
WP_W = 3456
Z_GQ, Z_GK, Z_GV, Z_GR, Z_DQ = 0, 128, 256, 512, 768
Z_RQ, Z_RK, Z_RV, Z_RG = 1280, 1536, 1792, 2048
Z_LA = 2304
Z_W = 2432

TOKEN_TILE = 512
FF_COLS = 256
SCAN_TILE = 512
RET_CHUNK = 128
GLA_SUB = 16
ATTN_TILE = 256
ATTN_CTX_TILE = 512

_MIB = 1024 * 1024


def _cparams(sem, vmem_mib):
    return pltpu.CompilerParams(dimension_semantics=sem, vmem_limit_bytes=vmem_mib * _MIB)


def _rms(x, g):
    return x * lax.rsqrt(jnp.mean(x * x, axis=-1, keepdims=True) + EPS) * g


def _sigmoid(x):
    return 1.0 / (1.0 + jnp.exp(-x))


def _const_spec(shape):
    n = len(shape)
    return pl.BlockSpec(shape, lambda *_: (0,) * n, pipeline_mode=pl.Buffered(1))


def _layer_spec(shape, layer):
    n = len(shape)
    return pl.BlockSpec((None,) + tuple(shape), lambda *_: (layer,) + (0,) * n, pipeline_mode=pl.Buffered(1))


def _ffn_math(x, gpre_ref, wup_ref, wdown_ref, gpost_ref, h_scr, act_scr):
    h_scr[...] = _rms(x, gpre_ref[...]).astype(BF16)
    for c in range(D_FF // FF_COLS):
        lo = c * FF_COLS
        gate = jnp.dot(h_scr[...], wup_ref[:, lo:lo + FF_COLS], preferred_element_type=F32)
        up = jnp.dot(h_scr[...], wup_ref[:, D_FF + lo:D_FF + lo + FF_COLS], preferred_element_type=F32)
        act_scr[:, lo:lo + FF_COLS] = (gate * _sigmoid(gate) * up).astype(BF16)
    y = jnp.dot(act_scr[...], wdown_ref[...], preferred_element_type=F32)
    return x + 0.5 * _rms(y, gpost_ref[...])


def _ffn_kernel(x_ref, gpre_ref, wup_ref, wdown_ref, gpost_ref, o_ref, h_scr, act_scr):
    o_ref[...] = _ffn_math(x_ref[...], gpre_ref, wup_ref, wdown_ref, gpost_ref, h_scr, act_scr)


def _ffn(x, gpre, wup, wdown, gpost, layer):
    n = x.shape[0]
    tm = min(TOKEN_TILE, n)
    return pl.pallas_call(
        _ffn_kernel,
        grid=(n // tm,),
        in_specs=[
            pl.BlockSpec((tm, D_MODEL), lambda i: (i, 0)),
            _layer_spec((1, D_MODEL), layer),
            _layer_spec((D_MODEL, 2 * D_FF), layer),
            _layer_spec((D_FF, D_MODEL), layer),
            _layer_spec((1, D_MODEL), layer),
        ],
        out_specs=pl.BlockSpec((tm, D_MODEL), lambda i: (i, 0)),
        out_shape=jax.ShapeDtypeStruct((n, D_MODEL), F32),
        scratch_shapes=[pltpu.VMEM((tm, D_MODEL), BF16), pltpu.VMEM((tm, D_FF), BF16)],
        compiler_params=_cparams(("parallel",), 48),
        name="ffn",
    )(x, gpre, wup, wdown, gpost)


def _inproj_kernel(*refs, n_alias, slot):
    x_ref, g_ref, w_ref, wa2_ref, ba_ref = refs[:5]
    z_ref, k4_ref, v4_ref, kb_ref, vb_ref, h_scr = refs[5 + n_alias:]
    for other in range(k4_ref.shape[0]):
        if other != slot:
            k4_ref[other] = jnp.zeros(k4_ref.shape[1:], F32)
            v4_ref[other] = jnp.zeros(v4_ref.shape[1:], F32)
    h_scr[...] = _rms(x_ref[...], g_ref[...]).astype(BF16)
    for lo in range(0, WP_DK, 256):
        z_ref[:, lo:lo + 256] = jnp.dot(h_scr[...], w_ref[:, lo:lo + 256], preferred_element_type=F32)
    for lo in range(0, WP_ALR - WP_RET, 256):
        z_ref[:, Z_RQ + lo:Z_RQ + lo + 256] = jnp.dot(
            h_scr[...], w_ref[:, WP_RET + lo:WP_RET + lo + 256], preferred_element_type=F32)
    a_lr = jnp.dot(h_scr[...], w_ref[:, WP_ALR:WP_ALR + LANES], preferred_element_type=F32)
    pre = jnp.dot(a_lr.astype(BF16), wa2_ref[...], preferred_element_type=F32) + ba_ref[...]
    log_sig = jnp.minimum(pre, 0.0) - jnp.log(1.0 + jnp.exp(-jnp.abs(pre)))
    z_ref[:, Z_LA:Z_LA + LANES] = log_sig / GLA_TAU
    hw = 2 * DK_D
    for lo in range(0, W_DIFF, 2 * hw):
        k_pair = jnp.dot(h_scr[...], w_ref[:, WP_DK + lo:WP_DK + lo + 2 * hw], preferred_element_type=F32)
        v_pair = jnp.dot(h_scr[...], w_ref[:, WP_DV + lo:WP_DV + lo + 2 * hw], preferred_element_type=F32)
        kb_ref[:, lo:lo + 2 * hw] = k_pair.astype(BF16)
        vb_ref[:, lo:lo + 2 * hw] = v_pair.astype(BF16)
        for i in range(2):
            k4_ref[slot, :, lo // hw + i, :] = k_pair[:, i * hw:(i + 1) * hw]
            v4_ref[slot, :, lo // hw + i, :] = v_pair[:, i * hw:(i + 1) * hw]


def _inproj(x, g, w, wa2, ba, layer, depth, kv_prev):
    n = x.shape[0]
    tm = min(TOKEN_TILE, n)
    hw = 2 * DK_D
    n_alias = 0 if kv_prev is None else 2
    kv_shape = jax.ShapeDtypeStruct((depth, n, H_D, hw), F32)
    if kv_prev is None:
        kv_spec = pl.BlockSpec((depth, tm, H_D, hw), lambda i: (0, i, 0, 0))
        slot = layer
    else:
        kv_spec = pl.BlockSpec((1, tm, H_D, hw), lambda i: (layer, i, 0, 0))
        slot = 0
    return pl.pallas_call(
        functools.partial(_inproj_kernel, n_alias=n_alias, slot=slot),
        grid=(n // tm,),
        in_specs=[
            pl.BlockSpec((tm, D_MODEL), lambda i: (i, 0)),
            _layer_spec((1, D_MODEL), layer),
            _layer_spec((D_MODEL, WP_W), layer),
            _layer_spec((LANES, LANES), layer),
            _layer_spec((1, LANES), layer),
        ] + [pl.BlockSpec(memory_space=pl.ANY)] * n_alias,
        out_specs=[
            pl.BlockSpec((tm, Z_W), lambda i: (i, 0)),
            kv_spec, kv_spec,
            pl.BlockSpec((tm, W_DIFF), lambda i: (i, 0)),
            pl.BlockSpec((tm, W_DIFF), lambda i: (i, 0)),
        ],
        out_shape=[
            jax.ShapeDtypeStruct((n, Z_W), F32),
            kv_shape, kv_shape,
            jax.ShapeDtypeStruct((n, W_DIFF), BF16),
            jax.ShapeDtypeStruct((n, W_DIFF), BF16),
        ],
        input_output_aliases={} if kv_prev is None else {5: 1, 6: 2},
        scratch_shapes=[pltpu.VMEM((tm, D_MODEL), BF16)],
        compiler_params=_cparams(("parallel",), 56),
        name="inproj",
    )(x, g, w, wa2, ba, *(() if kv_prev is None else kv_prev))


def _gla_kernel(q_ref, k_ref, v_ref, la_ref, s0_ref, o_ref, st_ref, st_scr, *, n_chunks):
    c = CHUNK

    @pl.when(pl.program_id(1) == 0)
    def _():
        st_scr[...] = s0_ref[0]

    r_i = lax.broadcasted_iota(jnp.int32, (c, c), 0)
    c_i = lax.broadcasted_iota(jnp.int32, (c, c), 1)
    tri = (r_i >= c_i).astype(BF16)
    head_k = lax.broadcasted_iota(jnp.int32, (H_A * DK_A, W_GLA), 0) // DK_A
    head_v = lax.broadcasted_iota(jnp.int32, (H_A * DK_A, W_GLA), 1) // DV_A
    sum_bcast = (head_k == head_v).astype(BF16)
    st_hv = lax.broadcasted_iota(jnp.int32, (W_GLA, H_A * DK_A), 0) // DV_A
    st_hk = lax.broadcasted_iota(jnp.int32, (W_GLA, H_A * DK_A), 1) // DK_A
    st_mask = st_hv == st_hk
    sub = lax.broadcasted_iota(jnp.int32, (SUBLANES, H_A * DK_A), 0)
    n_groups = c // SUBLANES
    sub_groups = GLA_SUB // SUBLANES
    head_q = lax.broadcasted_iota(jnp.int32, (GLA_SUB, H_A * DK_A), 1) // DK_A
    head_o = lax.broadcasted_iota(jnp.int32, (GLA_SUB, W_GLA), 1) // DV_A

    n_sub = c // GLA_SUB

    def front(ci):
        rows = slice(ci * c, (ci + 1) * c)
        q = q_ref[0, rows, :] * (DK_A ** -0.5)
        k = k_ref[0, rows, :]
        la = la_ref[0, rows, :]

        la1 = la.astype(BF16)
        rem = la - la1.astype(F32)
        la2 = rem.astype(BF16)
        la3 = (rem - la2.astype(F32)).astype(BF16)
        b3 = jnp.dot(tri, jnp.concatenate([la1, la2, la3], axis=1), preferred_element_type=F32)
        b = b3[:, :LANES] + b3[:, LANES:2 * LANES] + b3[:, 2 * LANES:]

        pieces = []
        for s in range(c):
            blk_end = (s // GLA_SUB + 1) * GLA_SUB
            r0 = (s // SUBLANES) * SUBLANES
            n_r = blk_end - r0
            e = jnp.exp(jnp.minimum(b[r0:blk_end, :] - b[s:s + 1, :], 0.0))
            p = q[r0:blk_end, :] * e * k[s:s + 1, :]
            if s % SUBLANES:
                first = jnp.where(sub >= (s % SUBLANES), p[:SUBLANES], 0.0)
                p = first if n_r == SUBLANES else jnp.concatenate([first, p[SUBLANES:]], axis=0)
            pieces.append(p)
        p_all = jnp.concatenate(pieces, axis=0).astype(BF16)
        w_all = jnp.dot(p_all, sum_bcast, preferred_element_type=F32)
        a_blocks = []
        for i in range(1, n_sub):
            lo = i * GLA_SUB
            ref = b[lo - 1:lo, :]
            q_i = q[lo:lo + GLA_SUB, :] * jnp.exp(b[lo:lo + GLA_SUB, :] - ref)
            k_i = (k[:lo, :] * jnp.exp(ref - b[:lo, :])).astype(BF16)
            q_heads = jnp.concatenate(
                [jnp.where(head_q == hh, q_i, 0.0) for hh in range(H_A)], axis=0).astype(BF16)
            a_blocks.append(lax.dot_general(q_heads, k_i, (((1,), (1,)), ((), ())),
                                            preferred_element_type=F32).astype(BF16))
        b_last = b[c - 1:c, :]
        return dict(w_all=w_all, a_blocks=a_blocks, q_dec=(q * jnp.exp(b)).astype(BF16),
                    k_dec=(k * jnp.exp(b_last - b)).astype(BF16), decay=jnp.exp(b_last))

    def middle(ci, f):
        v_bf = v_ref[0, ci * c:(ci + 1) * c, :].astype(BF16)
        f['r_blocks'] = [jnp.dot(a, v_bf[:(i + 1) * GLA_SUB, :], preferred_element_type=F32)
                         for i, a in enumerate(f.pop('a_blocks'))]
        f['upd'] = lax.dot_general(v_bf, f.pop('k_dec'), (((0,), (0,)), ((), ())), preferred_element_type=F32)
        return f

    def back(ci, f, st):
        v = v_ref[0, ci * c:(ci + 1) * c, :]
        w_all = f['w_all']
        acc = [None] * n_groups
        off = 0
        for s in range(c):
            v_s = v[s:s + 1, :]
            for g in range(s // SUBLANES, (s // GLA_SUB + 1) * GLA_SUB // SUBLANES):
                t = w_all[off:off + SUBLANES, :] * v_s
                acc[g] = t if acc[g] is None else acc[g] + t
                off += SUBLANES
        o_blocks = [jnp.concatenate(acc[i * sub_groups:(i + 1) * sub_groups], axis=0) for i in range(n_sub)]
        for i, r_i in enumerate(f['r_blocks']):
            for hh in range(H_A):
                o_blocks[i + 1] = o_blocks[i + 1] + jnp.where(
                    head_o == hh, r_i[hh * GLA_SUB:(hh + 1) * GLA_SUB], 0.0)
        o_inter = lax.dot_general(f['q_dec'], st.astype(BF16), (((1,), (1,)), ((), ())),
                                  preferred_element_type=F32)
        st_new = st * f['decay'] + jnp.where(st_mask, f['upd'], 0.0)
        return st_new, jnp.concatenate(o_blocks, axis=0) + o_inter

    st = st_scr[...]
    outs = []
    stage = {}
    for t in range(n_chunks + 2):
        if t < n_chunks:
            stage[t] = front(t)
        if 0 <= t - 1 < n_chunks:
            stage[t - 1] = middle(t - 1, stage[t - 1])
        if 0 <= t - 2 < n_chunks:
            st, o = back(t - 2, stage.pop(t - 2), st)
            outs.append(o)
    for ci in range(n_chunks):
        o_ref[0, ci * c:(ci + 1) * c, :] = outs[ci]
    st_scr[...] = st
    st_ref[0] = st


def _gla(z, s0t):
    b, t, _ = z.shape
    tb = min(SCAN_TILE, t)
    kw = H_A * DK_A
    return pl.pallas_call(
        functools.partial(_gla_kernel, n_chunks=tb // CHUNK),
        grid=(b, t // tb),
        in_specs=[
            pl.BlockSpec((1, tb, kw), lambda i, j: (i, j, Z_GQ // kw)),
            pl.BlockSpec((1, tb, kw), lambda i, j: (i, j, Z_GK // kw)),
            pl.BlockSpec((1, tb, W_GLA), lambda i, j: (i, j, Z_GV // W_GLA)),
            pl.BlockSpec((1, tb, kw), lambda i, j: (i, j, Z_LA // kw)),
            pl.BlockSpec((1, W_GLA, kw), lambda i, j: (i, 0, 0)),
        ],
        out_specs=[
            pl.BlockSpec((1, tb, W_GLA), lambda i, j: (i, j, 0)),
            pl.BlockSpec((1, W_GLA, kw), lambda i, j: (i, 0, 0)),
        ],
        out_shape=[
            jax.ShapeDtypeStruct((b, t, W_GLA), F32),
            jax.ShapeDtypeStruct((b, W_GLA, kw), F32),
        ],
        scratch_shapes=[pltpu.VMEM((W_GLA, kw), F32)],
        compiler_params=_cparams(("parallel", "arbitrary"), 32),
        name="gla",
    )(z, z, z, z, s0t)


def _swap_halves(x):
    lane = lax.broadcasted_iota(jnp.int32, (x.shape[0], LANES), 1)
    first_half = (lane % DK_R) < (DK_R // 2)
    cols = []
    for lo in range(0, x.shape[1], LANES):
        blk = x[:, lo:lo + LANES]
        up = pltpu.roll(blk, LANES - DK_R // 2, axis=1)
        down = pltpu.roll(blk, DK_R // 2, axis=1)
        cols.append(jnp.where(first_half, up, down))
    return jnp.concatenate(cols, axis=1)


def _ret_kernel(q_ref, k_ref, v_ref, cos_ref, sin_ref, dec_ref, qdec_ref, kdec_ref, cdec_ref, s0_ref,
                o_ref, st_ref, st_scr, *, n_chunks, c):
    @pl.when(pl.program_id(1) == 0)
    def _():
        st_scr[...] = s0_ref[0]

    head_l = lax.broadcasted_iota(jnp.int32, (c, W_RET), 1) // DK_R
    st_mask = (lax.broadcasted_iota(jnp.int32, (W_RET, W_RET), 0) // DV_R
               == lax.broadcasted_iota(jnp.int32, (W_RET, W_RET), 1) // DK_R)

    def front(ci):
        rows = slice(ci * c, (ci + 1) * c)
        cos = cos_ref[rows, :]
        sin = sin_ref[rows, :]
        q = q_ref[0, rows, :]
        k = k_ref[0, rows, :]
        q = q * cos + _swap_halves(q) * sin
        k = (k * cos + _swap_halves(k) * sin) * (DK_R ** -0.5)
        q_stack = jnp.concatenate(
            [jnp.where(head_l == h, q, 0.0) for h in range(H_R)], axis=0).astype(BF16)
        s = lax.dot_general(q_stack, k.astype(BF16), (((1,), (1,)), ((), ())),
                            preferred_element_type=F32)
        return dict(s=s, q_dec=(q * qdec_ref[...]).astype(BF16), k_dec=(k * kdec_ref[...]).astype(BF16))

    def middle(ci, f):
        v = v_ref[0, ci * c:(ci + 1) * c, :].astype(BF16)
        s = (f.pop('s') * dec_ref[...]).astype(BF16)
        f['r'] = jnp.dot(s, v, preferred_element_type=F32)
        f['upd'] = lax.dot_general(v, f.pop('k_dec'), (((0,), (0,)), ((), ())), preferred_element_type=F32)
        return f

    def back(f, st):
        r = f['r']
        o = jnp.where(head_l == 0, r[:c], 0.0)
        for h in range(1, H_R):
            o = o + jnp.where(head_l == h, r[h * c:(h + 1) * c], 0.0)
        o = o + lax.dot_general(f['q_dec'], st.astype(BF16), (((1,), (1,)), ((), ())),
                                preferred_element_type=F32)
        return st * cdec_ref[...] + jnp.where(st_mask, f['upd'], 0.0), o

    st = st_scr[...]
    outs = []
    stage = {}
    for t in range(n_chunks + 2):
        if t < n_chunks:
            stage[t] = front(t)
        if 0 <= t - 1 < n_chunks:
            stage[t - 1] = middle(t - 1, stage[t - 1])
        if 0 <= t - 2 < n_chunks:
            st, o = back(stage.pop(t - 2), st)
            outs.append(o)
    for ci in range(n_chunks):
        o_ref[0, ci * c:(ci + 1) * c, :] = outs[ci]
    st_scr[...] = st
    st_ref[0] = st


def _ret_tables(c):
    lg = jnp.log(1.0 - 2.0 ** (-5.0 - jnp.arange(H_R, dtype=F32)))
    idx = jnp.arange(c, dtype=F32)
    tri = idx[:, None] >= idx[None, :]
    dec = jnp.exp(jnp.where(tri[None], (idx[:, None] - idx[None, :])[None] * lg[:, None, None], -jnp.inf))
    q_dec = jnp.exp((idx[None, :] + 1.0) * lg[:, None])
    k_dec = jnp.exp((c - 1.0 - idx[None, :]) * lg[:, None])
    c_dec = jnp.exp(c * lg)
    return (dec.reshape(H_R * c, c),
            jnp.repeat(q_dec.T, DK_R, axis=1),
            jnp.repeat(k_dec.T, DK_R, axis=1),
            jnp.repeat(c_dec, DK_R)[None, :])


def _rope_tables(pos):
    half = DK_R // 2
    inv = ROPE_BASE ** (-jnp.arange(half, dtype=F32) / half)
    ang = pos.astype(F32)[:, None] * inv[None, :]
    cos = jnp.cos(ang)
    sin = jnp.sin(ang)
    cos_t = jnp.tile(jnp.concatenate([cos, cos], axis=1), (1, H_R))
    sin_t = jnp.tile(jnp.concatenate([-sin, sin], axis=1), (1, H_R))
    return cos_t, sin_t


def _ret(z, s0t, past_len):
    b, t, _ = z.shape
    tb = min(SCAN_TILE, t)
    c = min(RET_CHUNK, t)
    cos_t, sin_t = _rope_tables(past_len + jnp.arange(t))
    dec, q_dec, k_dec, c_dec = _ret_tables(c)
    return pl.pallas_call(
        functools.partial(_ret_kernel, n_chunks=tb // c, c=c),
        grid=(b, t // tb),
        in_specs=[
            pl.BlockSpec((1, tb, W_RET), lambda i, j: (i, j, Z_RQ // W_RET)),
            pl.BlockSpec((1, tb, W_RET), lambda i, j: (i, j, Z_RK // W_RET)),
            pl.BlockSpec((1, tb, W_RET), lambda i, j: (i, j, Z_RV // W_RET)),
            pl.BlockSpec((tb, W_RET), lambda i, j: (j, 0)),
            pl.BlockSpec((tb, W_RET), lambda i, j: (j, 0)),
            _const_spec((H_R * c, c)),
            _const_spec((c, W_RET)),
            _const_spec((c, W_RET)),
            _const_spec((1, W_RET)),
            pl.BlockSpec((1, W_RET, W_RET), lambda i, j: (i, 0, 0)),
        ],
        out_specs=[
            pl.BlockSpec((1, tb, W_RET), lambda i, j: (i, j, 0)),
            pl.BlockSpec((1, W_RET, W_RET), lambda i, j: (i, 0, 0)),
        ],
        out_shape=[
            jax.ShapeDtypeStruct((b, t, W_RET), F32),
            jax.ShapeDtypeStruct((b, W_RET, W_RET), F32),
        ],
        scratch_shapes=[pltpu.VMEM((W_RET, W_RET), F32)],
        compiler_params=_cparams(("parallel", "arbitrary"), 32),
        name="ret",
    )(z, z, z, cos_t, sin_t, dec, q_dec, k_dec, c_dec, s0t)


V_ROWS = DV_D + 16


LOG2E = 1.4426950408889634


def _stack_maps(q):
    lane = lax.broadcasted_iota(jnp.int32, q.shape, 1)
    q = q * (DK_D ** -0.5 * LOG2E)
    return jnp.concatenate(
        [jnp.where(lane < DK_D, q, 0.0), jnp.where(lane >= DK_D, q, 0.0)], axis=0).astype(BF16)


def _scores_t(k_tile, q_stack):
    return lax.dot_general(k_tile, q_stack, (((1,), (1,)), ((), ())), preferred_element_type=F32)


def _softmax_t(s, rel, shift, m_old):
    s = s + rel
    m_new = jnp.maximum(m_old, jnp.max(s, axis=0, keepdims=True) + shift)
    p = jnp.exp2(s - (m_new - shift)).astype(BF16)
    return m_new, p, jnp.exp2(m_old - m_new)


def _accumulate_t(acc, alpha, vt_tile, p):
    return alpha * acc + jnp.dot(vt_tile, p, preferred_element_type=F32)


def _attn_finish(acc, lam, g, out_scale, tq):
    o_t = acc[0:DV_D] / acc[DV_D:DV_D + 1]
    o = o_t[:, :tq] - lam * o_t[:, tq:]
    y_t = o * lax.rsqrt(jnp.mean(o * o, axis=0, keepdims=True) + EPS)
    return y_t.T * g * out_scale


def _with_ones(v_t):
    return jnp.concatenate([v_t, jnp.ones((V_ROWS - DV_D, v_t.shape[1]), BF16)], axis=0)


def _attn_prompt_kernel(lam_ref, slope_ref, q_ref, kb_ref, vb_ref, relc_ref, reld_ref, g_ref, o_ref,
                        vt_scr, s_a, s_b, p_a, p_b, acc_scr, *, tq, n_tiles, layer, out_scale):
    h = pl.program_id(1)
    for i in range(n_tiles):
        v_t = vb_ref[0, i * tq:(i + 1) * tq, :].astype(F32).T.astype(BF16)
        vt_scr[i] = _with_ones(v_t)
    slope2 = slope_ref[h] * LOG2E
    s_bufs = (s_a, s_b)
    p_bufs = (p_a, p_b)
    col_groups = range(0, 2 * tq, LANES)

    def softmax_all(s_ref, rel_ref, shift, m_old):
        return [_softmax_t(s_ref[:, lo:lo + LANES], rel_ref[0, :, lo:lo + LANES], shift, m_old[i])
                for i, lo in enumerate(col_groups)]

    for qi in range(n_tiles):
        q_stack = _stack_maps(q_ref[0, qi * tq:(qi + 1) * tq, :])

        def scores(j):
            return _scores_t(kb_ref[0, j * tq:(j + 1) * tq, :], q_stack)

        s_bufs[0][...] = scores(0)
        m = [jnp.full((1, LANES), NEG_INF, F32) for _ in col_groups]
        alpha_prev = None
        for j in range(qi):
            s_in, s_out = s_bufs[j % 2], s_bufs[1 - j % 2]
            p_in, p_out = p_bufs[1 - j % 2], p_bufs[j % 2]
            stats = softmax_all(s_in, relc_ref, -slope2 * float((qi - j) * tq), m)
            s_next = scores(j + 1)
            if j == 1:
                acc_new = jnp.dot(vt_scr[0], p_in[...], preferred_element_type=F32)
            elif j > 1:
                acc_new = _accumulate_t(acc_scr[...], alpha_prev, vt_scr[j - 1], p_in[...])
            for (m_new, p, alpha), lo in zip(stats, col_groups):
                p_out[:, lo:lo + LANES] = p
            m = [st[0] for st in stats]
            alpha_prev = jnp.concatenate([st[2] for st in stats], axis=1)
            s_out[...] = s_next
            if j >= 1:
                acc_scr[...] = acc_new
        stats = softmax_all(s_bufs[qi % 2], reld_ref, 0.0, m)
        p = jnp.concatenate([st[1] for st in stats], axis=1)
        alpha = jnp.concatenate([st[2] for st in stats], axis=1)
        if qi == 0:
            acc = jnp.dot(vt_scr[0], p, preferred_element_type=F32)
        else:
            if qi == 1:
                acc = jnp.dot(vt_scr[0], p_bufs[1 - qi % 2][...], preferred_element_type=F32)
            else:
                acc = _accumulate_t(acc_scr[...], alpha_prev, vt_scr[qi - 1], p_bufs[1 - qi % 2][...])
            acc = _accumulate_t(acc, alpha, vt_scr[qi], p)
        o_ref[0, qi * tq:(qi + 1) * tq, :] = _attn_finish(acc, lam_ref[layer], g_ref[...], out_scale, tq)


def _attn_tables(tk, tq):
    slopes = jnp.asarray([2.0 ** (-8.0 * (i + 1) / H_D) for i in range(H_D)], F32)
    slopes2 = slopes * LOG2E
    key = lax.broadcasted_iota(jnp.int32, (tk, 2 * tq), 0)
    qry = lax.broadcasted_iota(jnp.int32, (tk, 2 * tq), 1) % tq
    rel_ctx = slopes2[:, None, None] * (key - qry).astype(F32)[None]
    key_d = lax.broadcasted_iota(jnp.int32, (tq, 2 * tq), 0)
    qry_d = lax.broadcasted_iota(jnp.int32, (tq, 2 * tq), 1) % tq
    visible = (key_d // CHUNK) <= (qry_d // CHUNK)
    rel_diag = jnp.where(visible[None], -slopes2[:, None, None] * jnp.abs(qry_d - key_d).astype(F32)[None],
                         NEG_INF)
    return slopes, rel_ctx, rel_diag


def _attn_prompt(lam, z3, kb, vb, g, layer, out_scale):
    b, t, _ = z3.shape
    tq = min(ATTN_TILE, t)
    n_tiles = t // tq
    hw = 2 * DK_D
    slopes, rel_ctx, rel_diag = _attn_tables(tq, tq)
    smem = pl.BlockSpec(memory_space=pltpu.SMEM)
    return pl.pallas_call(
        functools.partial(_attn_prompt_kernel, tq=tq, n_tiles=n_tiles, layer=layer, out_scale=out_scale),
        grid=(b, H_D),
        in_specs=[
            smem, smem,
            pl.BlockSpec((1, t, hw), lambda i, h: (i, 0, Z_DQ // hw + h)),
            pl.BlockSpec((1, t, hw), lambda i, h: (i, 0, h)),
            pl.BlockSpec((1, t, DV_D), lambda i, h: (i, 0, h)),
            pl.BlockSpec((1, tq, 2 * tq), lambda i, h: (h, 0, 0)),
            pl.BlockSpec((1, tq, 2 * tq), lambda i, h: (h, 0, 0)),
            _layer_spec((1, DV_D), layer),
        ],
        out_specs=pl.BlockSpec((1, t, DV_D), lambda i, h: (i, 0, h)),
        out_shape=jax.ShapeDtypeStruct((b, t, W_DIFF), F32),
        scratch_shapes=[
            pltpu.VMEM((n_tiles, V_ROWS, tq), BF16),
            pltpu.VMEM((tq, 2 * tq), F32), pltpu.VMEM((tq, 2 * tq), F32),
            pltpu.VMEM((tq, 2 * tq), BF16), pltpu.VMEM((tq, 2 * tq), BF16),
            pltpu.VMEM((V_ROWS, 2 * tq), F32),
        ],
        compiler_params=_cparams(("parallel", "parallel"), 32),
        name="attn_prompt",
    )(lam, slopes, z3, kb, vb, rel_ctx, rel_diag, g)


def _attn_sample_kernel(lam_ref, slope_ref, q01_ref, q23_ref, kc_ref, vc_ref, kb_ref, vb_ref, relc_ref, reld_ref,
                        g_ref, o_ref, *, tq, tk, n_ctx, q_off, layer, out_scale):
    hw = 2 * DK_D
    for h in range(H_D):
        slope2 = slope_ref[h] * LOG2E
        q_ref = q01_ref if h < 2 else q23_ref
        q_stack = _stack_maps(q_ref[0, :, (h % 2) * hw:(h % 2 + 1) * hw])
        m = jnp.full((1, 2 * tq), NEG_INF, F32)
        acc = jnp.zeros((V_ROWS, 2 * tq), F32)
        for j in range(n_ctx):
            k_tile = kc_ref[j * tk:(j + 1) * tk, h, :].astype(BF16)
            v_t = vc_ref[j * tk:(j + 1) * tk, h, :].T.astype(BF16)
            shift = -slope2 * float(q_off - j * tk)
            m, p, alpha = _softmax_t(_scores_t(k_tile, q_stack), relc_ref[h], shift, m)
            acc = _accumulate_t(acc, alpha, _with_ones(v_t), p)
        v_t = vb_ref[0, :, h * hw:(h + 1) * hw].astype(F32).T.astype(BF16)
        s = _scores_t(kb_ref[0, :, h * hw:(h + 1) * hw], q_stack)
        _, p, alpha = _softmax_t(s, reld_ref[h], 0.0, m)
        acc = _accumulate_t(acc, alpha, _with_ones(v_t), p)
        o_ref[0, :, h * hw:(h + 1) * hw] = _attn_finish(acc, lam_ref[layer], g_ref[...], out_scale, tq)


def _attn_sample(lam, z3, cache_k, cache_v, layer, kb, vb, g, out_scale):
    b, t, _ = z3.shape
    t_past = cache_k.shape[2]
    tk = min(ATTN_CTX_TILE, t_past)
    slopes, rel_ctx, rel_diag = _attn_tables(tk, t)
    smem = pl.BlockSpec(memory_space=pltpu.SMEM)
    return pl.pallas_call(
        functools.partial(_attn_sample_kernel, tq=t, tk=tk, n_ctx=t_past // tk, q_off=t_past, layer=layer,
                          out_scale=out_scale),
        grid=(b,),
        in_specs=[
            smem, smem,
            pl.BlockSpec((1, t, W_DIFF // 2), lambda i: (i, 0, Z_DQ // (W_DIFF // 2))),
            pl.BlockSpec((1, t, W_DIFF // 2), lambda i: (i, 0, Z_DQ // (W_DIFF // 2) + 1)),
            pl.BlockSpec((None, None, t_past, H_D, 2 * DK_D), lambda i: (layer, i, 0, 0, 0)),
            pl.BlockSpec((None, None, t_past, H_D, DV_D), lambda i: (layer, i, 0, 0, 0)),
            pl.BlockSpec((1, t, W_DIFF), lambda i: (i, 0, 0)),
            pl.BlockSpec((1, t, W_DIFF), lambda i: (i, 0, 0)),
            _const_spec((H_D, tk, 2 * t)),
            _const_spec((H_D, t, 2 * t)),
            _layer_spec((1, DV_D), layer),
        ],
        out_specs=pl.BlockSpec((1, t, W_DIFF), lambda i: (i, 0, 0)),
        out_shape=jax.ShapeDtypeStruct((b, t, W_DIFF), F32),
        compiler_params=_cparams(("parallel",), 48),
        name="attn_sample",
    )(lam, slopes, z3, z3, cache_k, cache_v, kb, vb, rel_ctx, rel_diag, g)


def _seg_mean(x, avg):
    hi = x.astype(BF16)
    lo = (x - hi.astype(F32)).astype(BF16)
    return jnp.dot(jnp.concatenate([hi, lo], axis=1), avg, preferred_element_type=F32)


def _outproj_ffn_kernel(x_ref, oa_ref, gr_ref, od_ref, or_ref, rg_ref, ga_ref, gret_ref, w_ref, gmix_ref,
                        gpre_ref, wup_ref, wdown_ref, gpost_ref, o_ref, h_scr, act_scr):
    lane_r = lax.broadcasted_iota(jnp.int32, (2 * W_GLA, W_GLA), 0) % W_GLA // DV_A
    lane_c = lax.broadcasted_iota(jnp.int32, (2 * W_GLA, W_GLA), 1) // DV_A
    avg = jnp.where(lane_r == lane_c, 1.0 / DV_A, 0.0).astype(BF16)

    oa = oa_ref[...]
    gr = gr_ref[...]
    oa_n = oa * lax.rsqrt(_seg_mean(oa * oa, avg) + EPS) * ga_ref[...]
    h_scr[:, 0:W_GLA] = (oa_n * (gr * _sigmoid(gr))).astype(BF16)

    h_scr[:, W_GLA:W_GLA + W_DIFF] = od_ref[...].astype(BF16)

    orr = or_ref[...]
    rg = rg_ref[...]
    cen = orr - _seg_mean(orr, avg)
    or_n = cen * lax.rsqrt(_seg_mean(cen * cen, avg) + EPS) * gret_ref[...]
    h_scr[:, W_GLA + W_DIFF:] = (or_n * (rg * _sigmoid(rg))).astype(BF16)

    m = jnp.dot(h_scr[...], w_ref[...], preferred_element_type=F32)
    x = x_ref[...] + _rms(m, gmix_ref[...])
    o_ref[...] = _ffn_math(x, gpre_ref, wup_ref, wdown_ref, gpost_ref, h_scr, act_scr)


def _outproj_ffn(x, oa, z, od, orr, ga, gret, w, gmix, gpre, wup, wdown, gpost, layer):
    n = x.shape[0]
    tm = min(TOKEN_TILE, n)
    return pl.pallas_call(
        _outproj_ffn_kernel,
        grid=(n // tm,),
        in_specs=[
            pl.BlockSpec((tm, D_MODEL), lambda i: (i, 0)),
            pl.BlockSpec((tm, W_GLA), lambda i: (i, 0)),
            pl.BlockSpec((tm, W_GLA), lambda i: (i, Z_GR // W_GLA)),
            pl.BlockSpec((tm, W_DIFF), lambda i: (i, 0)),
            pl.BlockSpec((tm, W_RET), lambda i: (i, 0)),
            pl.BlockSpec((tm, W_RET), lambda i: (i, Z_RG // W_RET)),
            _layer_spec((1, W_GLA), layer),
            _layer_spec((1, W_RET), layer),
            _layer_spec((D_MODEL, D_MODEL), layer),
            _layer_spec((1, D_MODEL), layer),
            _layer_spec((1, D_MODEL), layer),
            _layer_spec((D_MODEL, 2 * D_FF), layer),
            _layer_spec((D_FF, D_MODEL), layer),
            _layer_spec((1, D_MODEL), layer),
        ],
        out_specs=pl.BlockSpec((tm, D_MODEL), lambda i: (i, 0)),
        out_shape=jax.ShapeDtypeStruct((n, D_MODEL), F32),
        scratch_shapes=[pltpu.VMEM((tm, D_MODEL), BF16), pltpu.VMEM((tm, D_FF), BF16)],
        compiler_params=_cparams(("parallel",), 52),
        name="outproj_ffn",
    )(x, oa, z, od, orr, z, ga, gret, w, gmix, gpre, wup, wdown, gpost)


def _regroup_w_in(w):
    pad = jnp.zeros(w.shape[:-1] + (LANES - GLA_RANK,), w.dtype)
    return jnp.concatenate([w[..., GLA_RANK:], w[..., :GLA_RANK], pad], axis=-1).astype(BF16)


def _state_to_blockdiag_t(s):
    b, h, dk, dv = s.shape
    eye = jnp.eye(h, dtype=s.dtype)
    return jnp.einsum('bhdv,hg->bhvgd', s, eye).reshape(b, h * dv, h * dk)


def _blockdiag_t_to_state(st, h, dk, dv):
    b = st.shape[0]
    blocks = st.reshape(b, h, dv, h, dk)
    diag = jnp.stack([blocks[:, i, :, i, :] for i in range(h)], axis=1)
    return diag.swapaxes(-1, -2)


def _layer(x, l, depth, p, past_k, past_v, s_gla0, s_ret0, kv_prev):
    b, t, _ = x.shape
    n = b * t
    x = _ffn(x.reshape(n, D_MODEL), p['ffn1_norm_pre'], p['ffn1_w_up'], p['ffn1_w_down'], p['ffn1_norm_post'], l)
    z, k4, v4, kb, vb = _inproj(x, p['mix_norm_pre'], p['w_in'], p['w_gla_a2'], p['b_gla_a'], l, depth, kv_prev)
    z3 = z.reshape(b, t, Z_W)
    kb3 = kb.reshape(b, t, W_DIFF)
    vb3 = vb.reshape(b, t, W_DIFF)

    o_a, st_gla = _gla(z3, s_gla0)

    out_scale = 1.0 - (0.8 - 0.6 * math.exp(-0.3 * l))
    if past_k is None:
        o_d = _attn_prompt(p['lam'], z3, kb3, vb3, p['diff_subln'], l, out_scale)
        past_len = 0
    else:
        o_d = _attn_sample(p['lam'], z3, past_k, past_v, l, kb3, vb3, p['diff_subln'], out_scale)
        past_len = past_k.shape[2]

    o_r, st_ret = _ret(z3, s_ret0, past_len)

    x = _outproj_ffn(x, o_a.reshape(n, W_GLA), z, o_d.reshape(n, W_DIFF), o_r.reshape(n, W_RET),
                     p['gla_norm'], p['ret_norm'], p['w_out'], p['mix_norm_post'],
                     p['ffn2_norm_pre'], p['ffn2_w_up'], p['ffn2_w_down'], p['ffn2_norm_post'], l)
    return x.reshape(b, t, D_MODEL), (k4, v4), st_gla, st_ret


def kernel(x_prompt, x_sample, cache_diff_k, cache_diff_v, state_gla, state_ret, ffn1_norm_pre, ffn1_w_up, ffn1_w_down, ffn1_norm_post, mix_norm_pre, w_in, w_gla_a2, b_gla_a, gla_norm, diff_lambda, diff_subln, ret_norm, w_out, mix_norm_post, ffn2_norm_pre, ffn2_w_up, ffn2_w_down, ffn2_norm_post):
    depth = w_in.shape[0]
    row = lambda a: a.reshape(depth, 1, -1)
    lam_p = diff_lambda.astype(F32)
    lam_init = jnp.asarray([0.8 - 0.6 * math.exp(-0.3 * l) for l in range(depth)], F32)
    lam = (jnp.exp(jnp.sum(lam_p[:, 0] * lam_p[:, 1], axis=-1))
           - jnp.exp(jnp.sum(lam_p[:, 2] * lam_p[:, 3], axis=-1)) + lam_init)
    p = {
        'ffn1_norm_pre': row(ffn1_norm_pre), 'ffn1_w_up': ffn1_w_up.astype(BF16),
        'ffn1_w_down': ffn1_w_down.astype(BF16), 'ffn1_norm_post': row(ffn1_norm_post),
        'mix_norm_pre': row(mix_norm_pre),
        'w_in': _regroup_w_in(w_in),
        'w_gla_a2': jnp.pad(w_gla_a2, ((0, 0), (0, LANES - GLA_RANK), (0, 0))).astype(BF16),
        'b_gla_a': row(b_gla_a), 'gla_norm': row(gla_norm), 'lam': lam,
        'diff_subln': row(diff_subln), 'ret_norm': row(ret_norm), 'w_out': w_out.astype(BF16),
        'mix_norm_post': row(mix_norm_post),
        'ffn2_norm_pre': row(ffn2_norm_pre), 'ffn2_w_up': ffn2_w_up.astype(BF16),
        'ffn2_w_down': ffn2_w_down.astype(BF16), 'ffn2_norm_post': row(ffn2_norm_post),
    }
    bp = x_prompt.shape[0]
    bs = x_sample.shape[0]
    zeros_gla = jnp.zeros((bp, W_GLA, H_A * DK_A), F32)
    zeros_ret = jnp.zeros((bp, W_RET, H_R * DK_R), F32)
    xp, xs = x_prompt, x_sample
    kv_p = kv_s = None
    st_p, st_s = [], []
    for l in range(depth):
        xp, kv_p, g_, r_ = _layer(xp, l, depth, p, None, None, zeros_gla, zeros_ret, kv_p)
        st_p.append((g_, r_))
        xs, kv_s, g_, r_ = _layer(xs, l, depth, p, cache_diff_k, cache_diff_v,
                                  _state_to_blockdiag_t(state_gla[l]), _state_to_blockdiag_t(state_ret[l]), kv_s)
        st_s.append((g_, r_))

    def states(sts):
        g = jnp.stack([_blockdiag_t_to_state(s[0], H_A, DK_A, DV_A) for s in sts])
        r = jnp.stack([_blockdiag_t_to_state(s[1], H_R, DK_R, DV_R) for s in sts])
        return g, r

    gp, rp = states(st_p)
    gs, rs = states(st_s)
    kv4 = lambda a, b: a.reshape(depth, b, -1, H_D, 2 * DK_D)
    return (xp, xs, kv4(kv_p[0], bp), kv4(kv_p[1], bp), gp, rp, kv4(kv_s[0], bs), kv4(kv_s[1], bs), gs, rs)
```

```python
import functools
import math

import jax
import jax.numpy as jnp
from jax import lax
from jax.experimental import pallas as pl
from jax.experimental.pallas import tpu as pltpu

F32 = jnp.float32
BF16 = jnp.bfloat16

D_MODEL = 1024
D_FF = 2816
EPS = 1e-6
NEG_INF = -1e30
CHUNK = 64

H_A, DK_A, DV_A = 4, 32, 64
GLA_RANK = 16
GLA_TAU = 16.0
H_D, DK_D, DV_D = 4, 64, 128
H_R, DK_R, DV_R = 4, 64, 64
ROPE_BASE = 10000.0

W_GLA = H_A * DV_A
W_DIFF = H_D * DV_D
W_RET = H_R * DV_R

LANES = 128
SUBLANES = 8

WP_DK = 1280
WP_DV = 1792
WP_RET = 2304
WP_ALR = 3328
WP_W = 3456
Z_GQ, Z_GK, Z_GV, Z_GR, Z_DQ = 0, 128, 256, 512, 768
Z_RQ, Z_RK, Z_RV, Z_RG = 1280, 1536, 1792, 2048
Z_LA = 2304
Z_W = 2432

TOKEN_TILE = 512
FFN_TILE = 1024
FF_COLS = 256
SCAN_TILE = 1024
RET_CHUNK = 128
GLA_SUB = 16
ATTN_TILE = 256
ATTN_CTX_TILE = 512

_MIB = 1024 * 1024


def _cparams(sem, vmem_mib):
    return pltpu.CompilerParams(dimension_semantics=sem, vmem_limit_bytes=vmem_mib * _MIB)


def _rms(x, g):
    return x * lax.rsqrt(jnp.mean(x * x, axis=-1, keepdims=True) + EPS) * g


def _sigmoid(x):
    return 1.0 / (1.0 + jnp.exp(-x))


def _const_spec(shape):
    n = len(shape)
    return pl.BlockSpec(shape, lambda *_: (0,) * n, pipeline_mode=pl.Buffered(1))


def _layer_spec(shape, layer):
    n = len(shape)
    return pl.BlockSpec((None,) + tuple(shape), lambda *_: (layer,) + (0,) * n, pipeline_mode=pl.Buffered(1))


def _ffn_math(x, gpre_ref, wup_ref, wdown_ref, gpost_ref, h_scr, act_scr):
    h_scr[...] = _rms(x, gpre_ref[...]).astype(BF16)
    for c in range(D_FF // FF_COLS):
        lo = c * FF_COLS
        gate = jnp.dot(h_scr[...], wup_ref[:, lo:lo + FF_COLS], preferred_element_type=F32)
        up = jnp.dot(h_scr[...], wup_ref[:, D_FF + lo:D_FF + lo + FF_COLS], preferred_element_type=F32)
        act_scr[:, lo:lo + FF_COLS] = (gate * _sigmoid(gate) * up).astype(BF16)
    y = jnp.dot(act_scr[...], wdown_ref[...], preferred_element_type=F32)
    return x + 0.5 * _rms(y, gpost_ref[...])


def _ffn_kernel(x_ref, gpre_ref, wup_ref, wdown_ref, gpost_ref, o_ref, h_scr, act_scr):
    o_ref[...] = _ffn_math(x_ref[...], gpre_ref, wup_ref, wdown_ref, gpost_ref, h_scr, act_scr)


def _ffn(x, gpre, wup, wdown, gpost, layer):
    n = x.shape[0]
    tm = min(FFN_TILE, n)
    return pl.pallas_call(
        _ffn_kernel,
        grid=(n // tm,),
        in_specs=[
            pl.BlockSpec((tm, D_MODEL), lambda i: (i, 0)),
            _layer_spec((1, D_MODEL), layer),
            _layer_spec((D_MODEL, 2 * D_FF), layer),
            _layer_spec((D_FF, D_MODEL), layer),
            _layer_spec((1, D_MODEL), layer),
        ],
        out_specs=pl.BlockSpec((tm, D_MODEL), lambda i: (i, 0)),
        out_shape=jax.ShapeDtypeStruct((n, D_MODEL), F32),
        scratch_shapes=[pltpu.VMEM((tm, D_MODEL), BF16), pltpu.VMEM((tm, D_FF), BF16)],
        compiler_params=_cparams(("parallel",), 48),
        name="ffn",
    )(x, gpre, wup, wdown, gpost)


def _inproj_kernel(*refs, n_alias, slot):
    x_ref, g_ref, w_ref, wa2_ref, ba_ref = refs[:5]
    z_ref, k4_ref, v4_ref, kb_ref, vb_ref, h_scr = refs[5 + n_alias:]
    for other in range(k4_ref.shape[0]):
        if other != slot:
            k4_ref[other] = jnp.zeros(k4_ref.shape[1:], F32)
            v4_ref[other] = jnp.zeros(v4_ref.shape[1:], F32)
    h_scr[...] = _rms(x_ref[...], g_ref[...]).astype(BF16)
    for lo in range(0, WP_DK, 256):
        z_ref[:, lo:lo + 256] = jnp.dot(h_scr[...], w_ref[:, lo:lo + 256], preferred_element_type=F32)
    for lo in range(0, WP_ALR - WP_RET, 256):
        z_ref[:, Z_RQ + lo:Z_RQ + lo + 256] = jnp.dot(
            h_scr[...], w_ref[:, WP_RET + lo:WP_RET + lo + 256], preferred_element_type=F32)
    a_lr = jnp.dot(h_scr[...], w_ref[:, WP_ALR:WP_ALR + LANES], preferred_element_type=F32)
    pre = jnp.dot(a_lr.astype(BF16), wa2_ref[...], preferred_element_type=F32) + ba_ref[...]
    log_sig = jnp.minimum(pre, 0.0) - jnp.log(1.0 + jnp.exp(-jnp.abs(pre)))
    z_ref[:, Z_LA:Z_LA + LANES] = log_sig / GLA_TAU
    hw = 2 * DK_D
    for lo in range(0, W_DIFF, 2 * hw):
        k_pair = jnp.dot(h_scr[...], w_ref[:, WP_DK + lo:WP_DK + lo + 2 * hw], preferred_element_type=F32)
        v_pair = jnp.dot(h_scr[...], w_ref[:, WP_DV + lo:WP_DV + lo + 2 * hw], preferred_element_type=F32)
        kb_ref[:, lo:lo + 2 * hw] = k_pair.astype(BF16)
        vb_ref[:, lo:lo + 2 * hw] = v_pair.astype(BF16)
        for i in range(2):
            k4_ref[slot, :, lo // hw + i, :] = k_pair[:, i * hw:(i + 1) * hw]
            v4_ref[slot, :, lo // hw + i, :] = v_pair[:, i * hw:(i + 1) * hw]


def _inproj(x, g, w, wa2, ba, layer, depth, kv_prev):
    n = x.shape[0]
    tm = min(TOKEN_TILE, n)
    hw = 2 * DK_D
    n_alias = 0 if kv_prev is None else 2
    kv_shape = jax.ShapeDtypeStruct((depth, n, H_D, hw), F32)
    if kv_prev is None:
        kv_spec = pl.BlockSpec((depth, tm, H_D, hw), lambda i: (0, i, 0, 0))
        slot = layer
    else:
        kv_spec = pl.BlockSpec((1, tm, H_D, hw), lambda i: (layer, i, 0, 0))
        slot = 0
    return pl.pallas_call(
        functools.partial(_inproj_kernel, n_alias=n_alias, slot=slot),
        grid=(n // tm,),
        in_specs=[
            pl.BlockSpec((tm, D_MODEL), lambda i: (i, 0)),
            _layer_spec((1, D_MODEL), layer),
            _layer_spec((D_MODEL, WP_W), layer),
            _layer_spec((LANES, LANES), layer),
            _layer_spec((1, LANES), layer),
        ] + [pl.BlockSpec(memory_space=pl.ANY)] * n_alias,
        out_specs=[
            pl.BlockSpec((tm, Z_W), lambda i: (i, 0)),
            kv_spec, kv_spec,
            pl.BlockSpec((tm, W_DIFF), lambda i: (i, 0)),
            pl.BlockSpec((tm, W_DIFF), lambda i: (i, 0)),
        ],
        out_shape=[
            jax.ShapeDtypeStruct((n, Z_W), F32),
            kv_shape, kv_shape,
            jax.ShapeDtypeStruct((n, W_DIFF), BF16),
            jax.ShapeDtypeStruct((n, W_DIFF), BF16),
        ],
        input_output_aliases={} if kv_prev is None else {5: 1, 6: 2},
        scratch_shapes=[pltpu.VMEM((tm, D_MODEL), BF16)],
        compiler_params=_cparams(("parallel",), 56),
        name="inproj",
    )(x, g, w, wa2, ba, *(() if kv_prev is None else kv_prev))


def _gla_kernel(q_ref, k_ref, v_ref, la_ref, s0_ref, o_ref, st_ref, st_scr, *, n_chunks):
    c = CHUNK

    @pl.when(pl.program_id(1) == 0)
    def _():
        st_scr[...] = s0_ref[0]

    r_i = lax.broadcasted_iota(jnp.int32, (c, c), 0)
    c_i = lax.broadcasted_iota(jnp.int32, (c, c), 1)
    tri = (r_i >= c_i).astype(BF16)
    head_k = lax.broadcasted_iota(jnp.int32, (H_A * DK_A, W_GLA), 0) // DK_A
    head_v = lax.broadcasted_iota(jnp.int32, (H_A * DK_A, W_GLA), 1) // DV_A
    sum_bcast = (head_k == head_v).astype(BF16)
    st_hv = lax.broadcasted_iota(jnp.int32, (W_GLA, H_A * DK_A), 0) // DV_A
    st_hk = lax.broadcasted_iota(jnp.int32, (W_GLA, H_A * DK_A), 1) // DK_A
    st_mask = st_hv == st_hk
    sub = lax.broadcasted_iota(jnp.int32, (SUBLANES, H_A * DK_A), 0)
    n_groups = c // SUBLANES
    sub_groups = GLA_SUB // SUBLANES
    head_q = lax.broadcasted_iota(jnp.int32, (GLA_SUB, H_A * DK_A), 1) // DK_A
    head_o = lax.broadcasted_iota(jnp.int32, (GLA_SUB, W_GLA), 1) // DV_A

    n_sub = c // GLA_SUB

    def front(ci):
        rows = slice(ci * c, (ci + 1) * c)
        q = q_ref[0, rows, :] * (DK_A ** -0.5)
        k = k_ref[0, rows, :]
        la = la_ref[0, rows, :]

        la1 = la.astype(BF16)
        rem = la - la1.astype(F32)
        la2 = rem.astype(BF16)
        la3 = (rem - la2.astype(F32)).astype(BF16)
        b3 = jnp.dot(tri, jnp.concatenate([la1, la2, la3], axis=1), preferred_element_type=F32)
        b = b3[:, :LANES] + b3[:, LANES:2 * LANES] + b3[:, 2 * LANES:]

        pieces = []
        for s in range(c):
            blk_end = (s // GLA_SUB + 1) * GLA_SUB
            r0 = (s // SUBLANES) * SUBLANES
            n_r = blk_end - r0
            e = jnp.exp(jnp.minimum(b[r0:blk_end, :] - b[s:s + 1, :], 0.0))
            p = q[r0:blk_end, :] * e * k[s:s + 1, :]
            if s % SUBLANES:
                first = jnp.where(sub >= (s % SUBLANES), p[:SUBLANES], 0.0)
                p = first if n_r == SUBLANES else jnp.concatenate([first, p[SUBLANES:]], axis=0)
            pieces.append(p)
        p_all = jnp.concatenate(pieces, axis=0).astype(BF16)
        w_all = jnp.dot(p_all, sum_bcast, preferred_element_type=F32)
        a_blocks = []
        for i in range(1, n_sub):
            lo = i * GLA_SUB
            ref = b[lo - 1:lo, :]
            q_i = q[lo:lo + GLA_SUB, :] * jnp.exp(b[lo:lo + GLA_SUB, :] - ref)
            k_i = (k[:lo, :] * jnp.exp(ref - b[:lo, :])).astype(BF16)
            q_heads = jnp.concatenate(
                [jnp.where(head_q == hh, q_i, 0.0) for hh in range(H_A)], axis=0).astype(BF16)
            a_blocks.append(lax.dot_general(q_heads, k_i, (((1,), (1,)), ((), ())),
                                            preferred_element_type=F32).astype(BF16))
        b_last = b[c - 1:c, :]
        return dict(w_all=w_all, a_blocks=a_blocks, q_dec=(q * jnp.exp(b)).astype(BF16),
                    k_dec=(k * jnp.exp(b_last - b)).astype(BF16), decay=jnp.exp(b_last))

    def middle(ci, f):
        v_bf = v_ref[0, ci * c:(ci + 1) * c, :].astype(BF16)
        f['r_blocks'] = [jnp.dot(a, v_bf[:(i + 1) * GLA_SUB, :], preferred_element_type=F32)
                         for i, a in enumerate(f.pop('a_blocks'))]
        f['upd'] = lax.dot_general(v_bf, f.pop('k_dec'), (((0,), (0,)), ((), ())), preferred_element_type=F32)
        return f

    def back(ci, f, st):
        v = v_ref[0, ci * c:(ci + 1) * c, :]
        w_all = f['w_all']
        acc = [None] * n_groups
        off = 0
        for s in range(c):
            v_s = v[s:s + 1, :]
            for g in range(s // SUBLANES, (s // GLA_SUB + 1) * GLA_SUB // SUBLANES):
                t = w_all[off:off + SUBLANES, :] * v_s
                acc[g] = t if acc[g] is None else acc[g] + t
                off += SUBLANES
        o_blocks = [jnp.concatenate(acc[i * sub_groups:(i + 1) * sub_groups], axis=0) for i in range(n_sub)]
        for i, r_i in enumerate(f['r_blocks']):
            for hh in range(H_A):
                o_blocks[i + 1] = o_blocks[i + 1] + jnp.where(
                    head_o == hh, r_i[hh * GLA_SUB:(hh + 1) * GLA_SUB], 0.0)
        o_inter = lax.dot_general(f['q_dec'], st.astype(BF16), (((1,), (1,)), ((), ())),
                                  preferred_element_type=F32)
        st_new = st * f['decay'] + jnp.where(st_mask, f['upd'], 0.0)
        return st_new, jnp.concatenate(o_blocks, axis=0) + o_inter

    st = st_scr[...]
    outs = []
    stage = {}
    for t in range(n_chunks + 2):
        if t < n_chunks:
            stage[t] = front(t)
        if 0 <= t - 1 < n_chunks:
            stage[t - 1] = middle(t - 1, stage[t - 1])
        if 0 <= t - 2 < n_chunks:
            st, o = back(t - 2, stage.pop(t - 2), st)
            outs.append(o)
    for ci in range(n_chunks):
        o_ref[0, ci * c:(ci + 1) * c, :] = outs[ci]
    st_scr[...] = st
    st_ref[0] = st


def _gla(z, s0t):
    b, t, _ = z.shape
    tb = min(SCAN_TILE, t)
    kw = H_A * DK_A
    return pl.pallas_call(
        functools.partial(_gla_kernel, n_chunks=tb // CHUNK),
        grid=(b, t // tb),
        in_specs=[
            pl.BlockSpec((1, tb, kw), lambda i, j: (i, j, Z_GQ // kw)),
            pl.BlockSpec((1, tb, kw), lambda i, j: (i, j, Z_GK // kw)),
            pl.BlockSpec((1, tb, W_GLA), lambda i, j: (i, j, Z_GV // W_GLA)),
            pl.BlockSpec((1, tb, kw), lambda i, j: (i, j, Z_LA // kw)),
            pl.BlockSpec((1, W_GLA, kw), lambda i, j: (i, 0, 0)),
        ],
        out_specs=[
            pl.BlockSpec((1, tb, W_GLA), lambda i, j: (i, j, 0)),
            pl.BlockSpec((1, W_GLA, kw), lambda i, j: (i, 0, 0)),
        ],
        out_shape=[
            jax.ShapeDtypeStruct((b, t, W_GLA), F32),
            jax.ShapeDtypeStruct((b, W_GLA, kw), F32),
        ],
        scratch_shapes=[pltpu.VMEM((W_GLA, kw), F32)],
        compiler_params=_cparams(("parallel", "arbitrary"), 32),
        name="gla",
    )(z, z, z, z, s0t)


def _swap_halves(x):
    lane = lax.broadcasted_iota(jnp.int32, (x.shape[0], LANES), 1)
    first_half = (lane % DK_R) < (DK_R // 2)
    cols = []
    for lo in range(0, x.shape[1], LANES):
        blk = x[:, lo:lo + LANES]
        up = pltpu.roll(blk, LANES - DK_R // 2, axis=1)
        down = pltpu.roll(blk, DK_R // 2, axis=1)
        cols.append(jnp.where(first_half, up, down))
    return jnp.concatenate(cols, axis=1)


def _ret_kernel(q_ref, k_ref, v_ref, cos_ref, sin_ref, dec_ref, qdec_ref, kdec_ref, cdec_ref, s0_ref,
                o_ref, st_ref, st_scr, *, n_chunks, c):
    @pl.when(pl.program_id(1) == 0)
    def _():
        st_scr[...] = s0_ref[0]

    head_l = lax.broadcasted_iota(jnp.int32, (c, W_RET), 1) // DK_R
    st_mask = (lax.broadcasted_iota(jnp.int32, (W_RET, W_RET), 0) // DV_R
               == lax.broadcasted_iota(jnp.int32, (W_RET, W_RET), 1) // DK_R)

    def front(ci):
        rows = slice(ci * c, (ci + 1) * c)
        cos = cos_ref[rows, :]
        sin = sin_ref[rows, :]
        q = q_ref[0, rows, :]
        k = k_ref[0, rows, :]
        q = q * cos + _swap_halves(q) * sin
        k = (k * cos + _swap_halves(k) * sin) * (DK_R ** -0.5)
        q_stack = jnp.concatenate(
            [jnp.where(head_l == h, q, 0.0) for h in range(H_R)], axis=0).astype(BF16)
        s = lax.dot_general(q_stack, k.astype(BF16), (((1,), (1,)), ((), ())),
                            preferred_element_type=F32)
        return dict(s=s, q_dec=(q * qdec_ref[...]).astype(BF16), k_dec=(k * kdec_ref[...]).astype(BF16))

    def middle(ci, f):
        v = v_ref[0, ci * c:(ci + 1) * c, :].astype(BF16)
        s = (f.pop('s') * dec_ref[...]).astype(BF16)
        f['r'] = jnp.dot(s, v, preferred_element_type=F32)
        f['upd'] = lax.dot_general(v, f.pop('k_dec'), (((0,), (0,)), ((), ())), preferred_element_type=F32)
        return f

    def back(f, st):
        r = f['r']
        o = jnp.where(head_l == 0, r[:c], 0.0)
        for h in range(1, H_R):
            o = o + jnp.where(head_l == h, r[h * c:(h + 1) * c], 0.0)
        o = o + lax.dot_general(f['q_dec'], st.astype(BF16), (((1,), (1,)), ((), ())),
                                preferred_element_type=F32)
        return st * cdec_ref[...] + jnp.where(st_mask, f['upd'], 0.0), o

    st = st_scr[...]
    outs = []
    stage = {}
    for t in range(n_chunks + 2):
        if t < n_chunks:
            stage[t] = front(t)
        if 0 <= t - 1 < n_chunks:
            stage[t - 1] = middle(t - 1, stage[t - 1])
        if 0 <= t - 2 < n_chunks:
            st, o = back(stage.pop(t - 2), st)
            outs.append(o)
    for ci in range(n_chunks):
        o_ref[0, ci * c:(ci + 1) * c, :] = outs[ci]
    st_scr[...] = st
    st_ref[0] = st


def _ret_tables(c):
    lg = jnp.log(1.0 - 2.0 ** (-5.0 - jnp.arange(H_R, dtype=F32)))
    idx = jnp.arange(c, dtype=F32)
    tri = idx[:, None] >= idx[None, :]
    dec = jnp.exp(jnp.where(tri[None], (idx[:, None] - idx[None, :])[None] * lg[:, None, None], -jnp.inf))
    q_dec = jnp.exp((idx[None, :] + 1.0) * lg[:, None])
    k_dec = jnp.exp((c - 1.0 - idx[None, :]) * lg[:, None])
    c_dec = jnp.exp(c * lg)
    return (dec.reshape(H_R * c, c),
            jnp.repeat(q_dec.T, DK_R, axis=1),
            jnp.repeat(k_dec.T, DK_R, axis=1),
            jnp.repeat(c_dec, DK_R)[None, :])


def _rope_tables(pos):
    half = DK_R // 2
    inv = ROPE_BASE ** (-jnp.arange(half, dtype=F32) / half)
    ang = pos.astype(F32)[:, None] * inv[None, :]
    cos = jnp.cos(ang)
    sin = jnp.sin(ang)
    cos_t = jnp.tile(jnp.concatenate([cos, cos], axis=1), (1, H_R))
    sin_t = jnp.tile(jnp.concatenate([-sin, sin], axis=1), (1, H_R))
    return cos_t, sin_t


def _ret(z, s0t, past_len):
    b, t, _ = z.shape
    tb = min(SCAN_TILE, t)
    c = min(RET_CHUNK, t)
    cos_t, sin_t = _rope_tables(past_len + jnp.arange(t))
    dec, q_dec, k_dec, c_dec = _ret_tables(c)
    return pl.pallas_call(
        functools.partial(_ret_kernel, n_chunks=tb // c, c=c),
        grid=(b, t // tb),
        in_specs=[
            pl.BlockSpec((1, tb, W_RET), lambda i, j: (i, j, Z_RQ // W_RET)),
            pl.BlockSpec((1, tb, W_RET), lambda i, j: (i, j, Z_RK // W_RET)),
            pl.BlockSpec((1, tb, W_RET), lambda i, j: (i, j, Z_RV // W_RET)),
            pl.BlockSpec((tb, W_RET), lambda i, j: (j, 0)),
            pl.BlockSpec((tb, W_RET), lambda i, j: (j, 0)),
            _const_spec((H_R * c, c)),
            _const_spec((c, W_RET)),
            _const_spec((c, W_RET)),
            _const_spec((1, W_RET)),
            pl.BlockSpec((1, W_RET, W_RET), lambda i, j: (i, 0, 0)),
        ],
        out_specs=[
            pl.BlockSpec((1, tb, W_RET), lambda i, j: (i, j, 0)),
            pl.BlockSpec((1, W_RET, W_RET), lambda i, j: (i, 0, 0)),
        ],
        out_shape=[
            jax.ShapeDtypeStruct((b, t, W_RET), F32),
            jax.ShapeDtypeStruct((b, W_RET, W_RET), F32),
        ],
        scratch_shapes=[pltpu.VMEM((W_RET, W_RET), F32)],
        compiler_params=_cparams(("parallel", "arbitrary"), 32),
        name="ret",
    )(z, z, z, cos_t, sin_t, dec, q_dec, k_dec, c_dec, s0t)


V_ROWS = DV_D + 16


LOG2E = 1.4426950408889634


def _stack_maps(q):
    lane = lax.broadcasted_iota(jnp.int32, q.shape, 1)
    q = q * (DK_D ** -0.5 * LOG2E)
    return jnp.concatenate(
        [jnp.where(lane < DK_D, q, 0.0), jnp.where(lane >= DK_D, q, 0.0)], axis=0).astype(BF16)


def _scores_t(k_tile, q_stack):
    return lax.dot_general(k_tile, q_stack, (((1,), (1,)), ((), ())), preferred_element_type=F32)


def _softmax_t(s, rel, shift, m_old):
    s = s + rel
    m_new = jnp.maximum(m_old, jnp.max(s, axis=0, keepdims=True) + shift)
    p = jnp.exp2(s - (m_new - shift)).astype(BF16)
    return m_new, p, jnp.exp2(m_old - m_new)


def _accumulate_t(acc, alpha, vt_tile, p):
    return alpha * acc + jnp.dot(vt_tile, p, preferred_element_type=F32)


def _attn_finish(acc, lam, g, out_scale, tq):
    o_t = acc[0:DV_D] / acc[DV_D:DV_D + 1]
    o = o_t[:, :tq] - lam * o_t[:, tq:]
    y_t = o * lax.rsqrt(jnp.mean(o * o, axis=0, keepdims=True) + EPS)
    return y_t.T * g * out_scale


def _with_ones(v_t):
    return jnp.concatenate([v_t, jnp.ones((V_ROWS - DV_D, v_t.shape[1]), BF16)], axis=0)


def _attn_prompt_kernel(lam_ref, slope_ref, q_ref, kb_ref, vb_ref, relc_ref, reld_ref, g_ref, o_ref,
                        vt_scr, s_a, s_b, p_a, p_b, acc_scr, *, tq, n_tiles, layer, out_scale):
    h = pl.program_id(1)
    for i in range(n_tiles):
        v_t = vb_ref[0, i * tq:(i + 1) * tq, :].astype(F32).T.astype(BF16)
        vt_scr[i] = _with_ones(v_t)
    slope2 = slope_ref[h] * LOG2E
    s_bufs = (s_a, s_b)
    p_bufs = (p_a, p_b)
    col_groups = range(0, 2 * tq, LANES)

    def softmax_all(s_ref, rel_ref, shift, m_old):
        return [_softmax_t(s_ref[:, lo:lo + LANES], rel_ref[0, :, lo:lo + LANES], shift, m_old[i])
                for i, lo in enumerate(col_groups)]

    for qi in range(n_tiles):
        q_stack = _stack_maps(q_ref[0, qi * tq:(qi + 1) * tq, :])

        def scores(j):
            return _scores_t(kb_ref[0, j * tq:(j + 1) * tq, :], q_stack)

        s_bufs[0][...] = scores(0)
        m = [jnp.full((1, LANES), NEG_INF, F32) for _ in col_groups]
        alpha_prev = None
        for j in range(qi):
            s_in, s_out = s_bufs[j % 2], s_bufs[1 - j % 2]
            p_in, p_out = p_bufs[1 - j % 2], p_bufs[j % 2]
            stats = softmax_all(s_in, relc_ref, -slope2 * float((qi - j) * tq), m)
            s_next = scores(j + 1)
            if j == 1:
                acc_new = jnp.dot(vt_scr[0], p_in[...], preferred_element_type=F32)
            elif j > 1:
                acc_new = _accumulate_t(acc_scr[...], alpha_prev, vt_scr[j - 1], p_in[...])
            for (m_new, p, alpha), lo in zip(stats, col_groups):
                p_out[:, lo:lo + LANES] = p
            m = [st[0] for st in stats]
            alpha_prev = jnp.concatenate([st[2] for st in stats], axis=1)
            s_out[...] = s_next
            if j >= 1:
                acc_scr[...] = acc_new
        stats = softmax_all(s_bufs[qi % 2], reld_ref, 0.0, m)
        p = jnp.concatenate([st[1] for st in stats], axis=1)
        alpha = jnp.concatenate([st[2] for st in stats], axis=1)
        if qi == 0:
            acc = jnp.dot(vt_scr[0], p, preferred_element_type=F32)
        else:
            if qi == 1:
                acc = jnp.dot(vt_scr[0], p_bufs[1 - qi % 2][...], preferred_element_type=F32)
            else:
                acc = _accumulate_t(acc_scr[...], alpha_prev, vt_scr[qi - 1], p_bufs[1 - qi % 2][...])
            acc = _accumulate_t(acc, alpha, vt_scr[qi], p)
        o_ref[0, qi * tq:(qi + 1) * tq, :] = _attn_finish(acc, lam_ref[layer], g_ref[...], out_scale, tq)


def _attn_tables(tk, tq):
    slopes = jnp.asarray([2.0 ** (-8.0 * (i + 1) / H_D) for i in range(H_D)], F32)
    slopes2 = slopes * LOG2E
    key = lax.broadcasted_iota(jnp.int32, (tk, 2 * tq), 0)
    qry = lax.broadcasted_iota(jnp.int32, (tk, 2 * tq), 1) % tq
    rel_ctx = slopes2[:, None, None] * (key - qry).astype(F32)[None]
    key_d = lax.broadcasted_iota(jnp.int32, (tq, 2 * tq), 0)
    qry_d = lax.broadcasted_iota(jnp.int32, (tq, 2 * tq), 1) % tq
    visible = (key_d // CHUNK) <= (qry_d // CHUNK)
    rel_diag = jnp.where(visible[None], -slopes2[:, None, None] * jnp.abs(qry_d - key_d).astype(F32)[None],
                         NEG_INF)
    return slopes, rel_ctx, rel_diag


def _attn_prompt(lam, z3, kb, vb, g, layer, out_scale):
    b, t, _ = z3.shape
    tq = min(ATTN_TILE, t)
    n_tiles = t // tq
    hw = 2 * DK_D
    slopes, rel_ctx, rel_diag = _attn_tables(tq, tq)
    smem = pl.BlockSpec(memory_space=pltpu.SMEM)
    return pl.pallas_call(
        functools.partial(_attn_prompt_kernel, tq=tq, n_tiles=n_tiles, layer=layer, out_scale=out_scale),
        grid=(b, H_D),
        in_specs=[
            smem, smem,
            pl.BlockSpec((1, t, hw), lambda i, h: (i, 0, Z_DQ // hw + h)),
            pl.BlockSpec((1, t, hw), lambda i, h: (i, 0, h)),
            pl.BlockSpec((1, t, DV_D), lambda i, h: (i, 0, h)),
            pl.BlockSpec((1, tq, 2 * tq), lambda i, h: (h, 0, 0)),
            pl.BlockSpec((1, tq, 2 * tq), lambda i, h: (h, 0, 0)),
            _layer_spec((1, DV_D), layer),
        ],
        out_specs=pl.BlockSpec((1, t, DV_D), lambda i, h: (i, 0, h)),
        out_shape=jax.ShapeDtypeStruct((b, t, W_DIFF), F32),
        scratch_shapes=[
            pltpu.VMEM((n_tiles, V_ROWS, tq), BF16),
            pltpu.VMEM((tq, 2 * tq), F32), pltpu.VMEM((tq, 2 * tq), F32),
            pltpu.VMEM((tq, 2 * tq), BF16), pltpu.VMEM((tq, 2 * tq), BF16),
            pltpu.VMEM((V_ROWS, 2 * tq), F32),
        ],
        compiler_params=_cparams(("parallel", "parallel"), 32),
        name="attn_prompt",
    )(lam, slopes, z3, kb, vb, rel_ctx, rel_diag, g)


def _attn_sample_kernel(lam_ref, slope_ref, q01_ref, q23_ref, kc_hbm, vc_hbm, kb_ref, vb_ref, relc_ref, reld_ref,
                        g_ref, o_ref, k_buf, v_buf, sem, *, tq, tk, n_ctx, q_off, layer, out_scale):
    hw = 2 * DK_D
    row = pl.program_id(0)

    def head_copies(h, slot):
        return (pltpu.make_async_copy(kc_hbm.at[layer, row, :, h, :], k_buf.at[slot], sem.at[0, slot]),
                pltpu.make_async_copy(vc_hbm.at[layer, row, :, h, :], v_buf.at[slot], sem.at[1, slot]))

    for cp in head_copies(0, 0):
        cp.start()
    for h in range(H_D):
        slot = h % 2
        if h + 1 < H_D:
            for cp in head_copies(h + 1, 1 - slot):
                cp.start()
        for cp in head_copies(h, slot):
            cp.wait()
        slope2 = slope_ref[h] * LOG2E
        q_ref = q01_ref if h < 2 else q23_ref
        q_stack = _stack_maps(q_ref[0, :, (h % 2) * hw:(h % 2 + 1) * hw])
        m = jnp.full((1, 2 * tq), NEG_INF, F32)
        acc = jnp.zeros((V_ROWS, 2 * tq), F32)
        for j in range(n_ctx):
            k_tile = k_buf[slot, j * tk:(j + 1) * tk, :].astype(BF16)
            v_t = v_buf[slot, j * tk:(j + 1) * tk, :].T.astype(BF16)
            shift = -slope2 * float(q_off - j * tk)
            m, p, alpha = _softmax_t(_scores_t(k_tile, q_stack), relc_ref[h], shift, m)
            acc = _accumulate_t(acc, alpha, _with_ones(v_t), p)
        v_t = vb_ref[0, :, h * hw:(h + 1) * hw].astype(F32).T.astype(BF16)
        s = _scores_t(kb_ref[0, :, h * hw:(h + 1) * hw], q_stack)
        _, p, alpha = _softmax_t(s, reld_ref[h], 0.0, m)
        acc = _accumulate_t(acc, alpha, _with_ones(v_t), p)
        o_ref[0, :, h * hw:(h + 1) * hw] = _attn_finish(acc, lam_ref[layer], g_ref[...], out_scale, tq)


def _attn_sample(lam, z3, cache_k, cache_v, layer, kb, vb, g, out_scale):
    b, t, _ = z3.shape
    t_past = cache_k.shape[2]
    tk = min(ATTN_CTX_TILE, t_past)
    slopes, rel_ctx, rel_diag = _attn_tables(tk, t)
    smem = pl.BlockSpec(memory_space=pltpu.SMEM)
    return pl.pallas_call(
        functools.partial(_attn_sample_kernel, tq=t, tk=tk, n_ctx=t_past // tk, q_off=t_past, layer=layer,
                          out_scale=out_scale),
        grid=(b,),
        in_specs=[
            smem, smem,
            pl.BlockSpec((1, t, W_DIFF // 2), lambda i: (i, 0, Z_DQ // (W_DIFF // 2))),
            pl.BlockSpec((1, t, W_DIFF // 2), lambda i: (i, 0, Z_DQ // (W_DIFF // 2) + 1)),
            pl.BlockSpec(memory_space=pl.ANY),
            pl.BlockSpec(memory_space=pl.ANY),
            pl.BlockSpec((1, t, W_DIFF), lambda i: (i, 0, 0)),
            pl.BlockSpec((1, t, W_DIFF), lambda i: (i, 0, 0)),
            _const_spec((H_D, tk, 2 * t)),
            _const_spec((H_D, t, 2 * t)),
            _layer_spec((1, DV_D), layer),
        ],
        out_specs=pl.BlockSpec((1, t, W_DIFF), lambda i: (i, 0, 0)),
        out_shape=jax.ShapeDtypeStruct((b, t, W_DIFF), F32),
        scratch_shapes=[
            pltpu.VMEM((2, t_past, 2 * DK_D), F32),
            pltpu.VMEM((2, t_past, DV_D), F32),
            pltpu.SemaphoreType.DMA((2, 2)),
        ],
        compiler_params=_cparams(("parallel",), 32),
        name="attn_sample",
    )(lam, slopes, z3, z3, cache_k, cache_v, kb, vb, rel_ctx, rel_diag, g)


def _seg_mean(x, avg):
    hi = x.astype(BF16)
    lo = (x - hi.astype(F32)).astype(BF16)
    return jnp.dot(jnp.concatenate([hi, lo], axis=1), avg, preferred_element_type=F32)


def _outproj_ffn_kernel(x_ref, oa_ref, gr_ref, od_ref, or_ref, rg_ref, ga_ref, gret_ref, w_ref, gmix_ref,
                        gpre_ref, wup_ref, wdown_ref, gpost_ref, o_ref, h_scr, act_scr):
    lane_r = lax.broadcasted_iota(jnp.int32, (2 * W_GLA, W_GLA), 0) % W_GLA // DV_A
    lane_c = lax.broadcasted_iota(jnp.int32, (2 * W_GLA, W_GLA), 1) // DV_A
    avg = jnp.where(lane_r == lane_c, 1.0 / DV_A, 0.0).astype(BF16)

    oa = oa_ref[...]
    gr = gr_ref[...]
    oa_n = oa * lax.rsqrt(_seg_mean(oa * oa, avg) + EPS) * ga_ref[...]
    h_scr[:, 0:W_GLA] = (oa_n * (gr * _sigmoid(gr))).astype(BF16)

    h_scr[:, W_GLA:W_GLA + W_DIFF] = od_ref[...].astype(BF16)

    orr = or_ref[...]
    rg = rg_ref[...]
    cen = orr - _seg_mean(orr, avg)
    or_n = cen * lax.rsqrt(_seg_mean(cen * cen, avg) + EPS) * gret_ref[...]
    h_scr[:, W_GLA + W_DIFF:] = (or_n * (rg * _sigmoid(rg))).astype(BF16)

    m = jnp.dot(h_scr[...], w_ref[...], preferred_element_type=F32)
    x = x_ref[...] + _rms(m, gmix_ref[...])
    o_ref[...] = _ffn_math(x, gpre_ref, wup_ref, wdown_ref, gpost_ref, h_scr, act_scr)


def _outproj_ffn(x, oa, z, od, orr, ga, gret, w, gmix, gpre, wup, wdown, gpost, layer):
    n = x.shape[0]
    tm = min(TOKEN_TILE, n)
    return pl.pallas_call(
        _outproj_ffn_kernel,
        grid=(n // tm,),
        in_specs=[
            pl.BlockSpec((tm, D_MODEL), lambda i: (i, 0)),
            pl.BlockSpec((tm, W_GLA), lambda i: (i, 0)),
            pl.BlockSpec((tm, W_GLA), lambda i: (i, Z_GR // W_GLA)),
            pl.BlockSpec((tm, W_DIFF), lambda i: (i, 0)),
            pl.BlockSpec((tm, W_RET), lambda i: (i, 0)),
            pl.BlockSpec((tm, W_RET), lambda i: (i, Z_RG // W_RET)),
            _layer_spec((1, W_GLA), layer),
            _layer_spec((1, W_RET), layer),
            _layer_spec((D_MODEL, D_MODEL), layer),
            _layer_spec((1, D_MODEL), layer),
            _layer_spec((1, D_MODEL), layer),
            _layer_spec((D_MODEL, 2 * D_FF), layer),
            _layer_spec((D_FF, D_MODEL), layer),
            _layer_spec((1, D_MODEL), layer),
        ],
        out_specs=pl.BlockSpec((tm, D_MODEL), lambda i: (i, 0)),
        out_shape=jax.ShapeDtypeStruct((n, D_MODEL), F32),
        scratch_shapes=[pltpu.VMEM((tm, D_MODEL), BF16), pltpu.VMEM((tm, D_FF), BF16)],
        compiler_params=_cparams(("parallel",), 52),
        name="outproj_ffn",
    )(x, oa, z, od, orr, z, ga, gret, w, gmix, gpre, wup, wdown, gpost)


def _regroup_w_in(w):
    pad = jnp.zeros(w.shape[:-1] + (LANES - GLA_RANK,), w.dtype)
    return jnp.concatenate([w[..., GLA_RANK:], w[..., :GLA_RANK], pad], axis=-1).astype(BF16)


def _state_to_blockdiag_t(s):
    b, h, dk, dv = s.shape
    eye = jnp.eye(h, dtype=s.dtype)
    return jnp.einsum('bhdv,hg->bhvgd', s, eye).reshape(b, h * dv, h * dk)


def _blockdiag_t_to_state(st, h, dk, dv):
    b = st.shape[0]
    blocks = st.reshape(b, h, dv, h, dk)
    diag = jnp.stack([blocks[:, i, :, i, :] for i in range(h)], axis=1)
    return diag.swapaxes(-1, -2)


def _layer(x, l, depth, p, past_k, past_v, s_gla0, s_ret0, kv_prev):
    b, t, _ = x.shape
    n = b * t
    x = _ffn(x.reshape(n, D_MODEL), p['ffn1_norm_pre'], p['ffn1_w_up'], p['ffn1_w_down'], p['ffn1_norm_post'], l)
    z, k4, v4, kb, vb = _inproj(x, p['mix_norm_pre'], p['w_in'], p['w_gla_a2'], p['b_gla_a'], l, depth, kv_prev)
    z3 = z.reshape(b, t, Z_W)
    kb3 = kb.reshape(b, t, W_DIFF)
    vb3 = vb.reshape(b, t, W_DIFF)

    o_a, st_gla = _gla(z3, s_gla0)

    out_scale = 1.0 - (0.8 - 0.6 * math.exp(-0.3 * l))
    if past_k is None:
        o_d = _attn_prompt(p['lam'], z3, kb3, vb3, p['diff_subln'], l, out_scale)
        past_len = 0
    else:
        o_d = _attn_sample(p['lam'], z3, past_k, past_v, l, kb3, vb3, p['diff_subln'], out_scale)
        past_len = past_k.shape[2]

    o_r, st_ret = _ret(z3, s_ret0, past_len)

    x = _outproj_ffn(x, o_a.reshape(n, W_GLA), z, o_d.reshape(n, W_DIFF), o_r.reshape(n, W_RET),
                     p['gla_norm'], p['ret_norm'], p['w_out'], p['mix_norm_post'],
                     p['ffn2_norm_pre'], p['ffn2_w_up'], p['ffn2_w_down'], p['ffn2_norm_post'], l)
    return x.reshape(b, t, D_MODEL), (k4, v4), st_gla, st_ret


def kernel(x_prompt, x_sample, cache_diff_k, cache_diff_v, state_gla, state_ret, ffn1_norm_pre, ffn1_w_up, ffn1_w_down, ffn1_norm_post, mix_norm_pre, w_in, w_gla_a2, b_gla_a, gla_norm, diff_lambda, diff_subln, ret_norm, w_out, mix_norm_post, ffn2_norm_pre, ffn2_w_up, ffn2_w_down, ffn2_norm_post):
    depth = w_in.shape[0]
    row = lambda a: a.reshape(depth, 1, -1)
    lam_p = diff_lambda.astype(F32)
    lam_init = jnp.asarray([0.8 - 0.6 * math.exp(-0.3 * l) for l in range(depth)], F32)
    lam = (jnp.exp(jnp.sum(lam_p[:, 0] * lam_p[:, 1], axis=-1))
           - jnp.exp(jnp.sum(lam_p[:, 2] * lam_p[:, 3], axis=-1)) + lam_init)
    p = {
        'ffn1_norm_pre': row(ffn1_norm_pre), 'ffn1_w_up': ffn1_w_up.astype(BF16),
        'ffn1_w_down': ffn1_w_down.astype(BF16), 'ffn1_norm_post': row(ffn1_norm_post),
        'mix_norm_pre': row(mix_norm_pre),
        'w_in': _regroup_w_in(w_in),
        'w_gla_a2': jnp.pad(w_gla_a2, ((0, 0), (0, LANES - GLA_RANK), (0, 0))).astype(BF16),
        'b_gla_a': row(b_gla_a), 'gla_norm': row(gla_norm), 'lam': lam,
        'diff_subln': row(diff_subln), 'ret_norm': row(ret_norm), 'w_out': w_out.astype(BF16),
        'mix_norm_post': row(mix_norm_post),
        'ffn2_norm_pre': row(ffn2_norm_pre), 'ffn2_w_up': ffn2_w_up.astype(BF16),
        'ffn2_w_down': ffn2_w_down.astype(BF16), 'ffn2_norm_post': row(ffn2_norm_post),
    }
    bp = x_prompt.shape[0]
    bs = x_sample.shape[0]
    zeros_gla = jnp.zeros((bp, W_GLA, H_A * DK_A), F32)
    zeros_ret = jnp.zeros((bp, W_RET, H_R * DK_R), F32)
    xp, xs = x_prompt, x_sample
    kv_p = kv_s = None
    st_p, st_s = [], []
    for l in range(depth):
        xp, kv_p, g_, r_ = _layer(xp, l, depth, p, None, None, zeros_gla, zeros_ret, kv_p)
        st_p.append((g_, r_))
        xs, kv_s, g_, r_ = _layer(xs, l, depth, p, cache_diff_k, cache_diff_v,
                                  _state_to_blockdiag_t(state_gla[l]), _state_to_blockdiag_t(state_ret[l]), kv_s)
        st_s.append((g_, r_))

    def states(sts):
        g = jnp.stack([_blockdiag_t_to_state(s[0], H_A, DK_A, DV_A) for s in sts])
        r = jnp.stack([_blockdiag_t_to_state(s[1], H_R, DK_R, DV_R) for s in sts])
        return g, r

    gp, rp = states(st_p)
    gs, rs = states(st_s)
    kv4 = lambda a, b: a.reshape(depth, b, -1, H_D, 2 * DK_D)
    return (xp, xs, kv4(kv_p[0], bp), kv4(kv_p[1], bp), gp, rp, kv4(kv_s[0], bs), kv4(kv_s[1], bs), gs, rs)
```

```python
import functools
import math

import jax
import jax.numpy as jnp
from jax import lax
from jax.experimental import pallas as pl
from jax.experimental.pallas import tpu as pltpu

F32 = jnp.float32
BF16 = jnp.bfloat16

D_MODEL = 1024
D_FF = 2816
EPS = 1e-6
NEG_INF = -1e30
CHUNK = 64

H_A, DK_A, DV_A = 4, 32, 64
GLA_RANK = 16
GLA_TAU = 16.0
H_D, DK_D, DV_D = 4, 64, 128
H_R, DK_R, DV_R = 4, 64, 64
ROPE_BASE = 10000.0

W_GLA = H_A * DV_A
W_DIFF = H_D * DV_D
W_RET = H_R * DV_R

LANES = 128
SUBLANES = 8

WP_DK = 1280
WP_DV = 1792
WP_RET = 2304
WP_ALR = 3328
WP_W = 3456
Z_GQ, Z_GK, Z_GV, Z_GR, Z_DQ = 0, 128, 256, 512, 768
Z_RQ, Z_RK, Z_RV, Z_RG = 1280, 1536, 1792, 2048
Z_LA = 2304
Z_W = 2432

TOKEN_TILE = 512
FFN_TILE = 1024
FF_COLS = 256
SCAN_TILE = 1024
RET_CHUNK = 128
GLA_SUB = 16
ATTN_TILE = 256
ATTN_CTX_TILE = 512

_MIB = 1024 * 1024


def _cparams(sem, vmem_mib):
    return pltpu.CompilerParams(dimension_semantics=sem, vmem_limit_bytes=vmem_mib * _MIB)


def _rms(x, g):
    return x * lax.rsqrt(jnp.mean(x * x, axis=-1, keepdims=True) + EPS) * g


def _sigmoid(x):
    return 1.0 / (1.0 + jnp.exp(-x))


def _const_spec(shape):
    n = len(shape)
    return pl.BlockSpec(shape, lambda *_: (0,) * n, pipeline_mode=pl.Buffered(1))


def _layer_spec(shape, layer):
    n = len(shape)
    return pl.BlockSpec((None,) + tuple(shape), lambda *_: (layer,) + (0,) * n, pipeline_mode=pl.Buffered(1))


def _ffn_math(x, gpre_ref, wup_ref, wdown_ref, gpost_ref, h_scr, act_scr):
    h_scr[...] = _rms(x, gpre_ref[...]).astype(BF16)
    for c in range(D_FF // FF_COLS):
        lo = c * FF_COLS
        gate = jnp.dot(h_scr[...], wup_ref[:, lo:lo + FF_COLS], preferred_element_type=F32)
        up = jnp.dot(h_scr[...], wup_ref[:, D_FF + lo:D_FF + lo + FF_COLS], preferred_element_type=F32)
        act_scr[:, lo:lo + FF_COLS] = (gate * _sigmoid(gate) * up).astype(BF16)
    y = jnp.dot(act_scr[...], wdown_ref[...], preferred_element_type=F32)
    return x + 0.5 * _rms(y, gpost_ref[...])


def _ffn_kernel(x_ref, gpre_ref, wup_ref, wdown_ref, gpost_ref, o_ref, h_scr, act_scr):
    o_ref[...] = _ffn_math(x_ref[...], gpre_ref, wup_ref, wdown_ref, gpost_ref, h_scr, act_scr)


def _ffn(x, gpre, wup, wdown, gpost, layer):
    n = x.shape[0]
    tm = min(FFN_TILE, n)
    return pl.pallas_call(
        _ffn_kernel,
        grid=(n // tm,),
        in_specs=[
            pl.BlockSpec((tm, D_MODEL), lambda i: (i, 0)),
            _layer_spec((1, D_MODEL), layer),
            _layer_spec((D_MODEL, 2 * D_FF), layer),
            _layer_spec((D_FF, D_MODEL), layer),
            _layer_spec((1, D_MODEL), layer),
        ],
        out_specs=pl.BlockSpec((tm, D_MODEL), lambda i: (i, 0)),
        out_shape=jax.ShapeDtypeStruct((n, D_MODEL), F32),
        scratch_shapes=[pltpu.VMEM((tm, D_MODEL), BF16), pltpu.VMEM((tm, D_FF), BF16)],
        compiler_params=_cparams(("parallel",), 48),
        name="ffn",
    )(x, gpre, wup, wdown, gpost)


def _inproj_kernel(*refs, n_alias, slot):
    x_ref, g_ref, w_ref, wa2_ref, ba_ref = refs[:5]
    z_ref, k4_ref, v4_ref, kb_ref, vb_ref, h_scr = refs[5 + n_alias:]
    for other in range(k4_ref.shape[0]):
        if other != slot:
            k4_ref[other] = jnp.zeros(k4_ref.shape[1:], F32)
            v4_ref[other] = jnp.zeros(v4_ref.shape[1:], F32)
    h_scr[...] = _rms(x_ref[...], g_ref[...]).astype(BF16)
    for lo in range(0, WP_DK, 256):
        z_ref[:, lo:lo + 256] = jnp.dot(h_scr[...], w_ref[:, lo:lo + 256], preferred_element_type=F32)
    for lo in range(0, WP_ALR - WP_RET, 256):
        z_ref[:, Z_RQ + lo:Z_RQ + lo + 256] = jnp.dot(
            h_scr[...], w_ref[:, WP_RET + lo:WP_RET + lo + 256], preferred_element_type=F32)
    a_lr = jnp.dot(h_scr[...], w_ref[:, WP_ALR:WP_ALR + LANES], preferred_element_type=F32)
    pre = jnp.dot(a_lr.astype(BF16), wa2_ref[...], preferred_element_type=F32) + ba_ref[...]
    log_sig = jnp.minimum(pre, 0.0) - jnp.log(1.0 + jnp.exp(-jnp.abs(pre)))
    z_ref[:, Z_LA:Z_LA + LANES] = log_sig / GLA_TAU
    hw = 2 * DK_D
    for lo in range(0, W_DIFF, 2 * hw):
        k_pair = jnp.dot(h_scr[...], w_ref[:, WP_DK + lo:WP_DK + lo + 2 * hw], preferred_element_type=F32)
        v_pair = jnp.dot(h_scr[...], w_ref[:, WP_DV + lo:WP_DV + lo + 2 * hw], preferred_element_type=F32)
        kb_ref[:, lo:lo + 2 * hw] = k_pair.astype(BF16)
        vb_ref[:, lo:lo + 2 * hw] = v_pair.astype(BF16)
        for i in range(2):
            k4_ref[slot, :, lo // hw + i, :] = k_pair[:, i * hw:(i + 1) * hw]
            v4_ref[slot, :, lo // hw + i, :] = v_pair[:, i * hw:(i + 1) * hw]


def _inproj(x, g, w, wa2, ba, layer, depth, kv_prev):
    n = x.shape[0]
    tm = min(TOKEN_TILE, n)
    hw = 2 * DK_D
    n_alias = 0 if kv_prev is None else 2
    kv_shape = jax.ShapeDtypeStruct((depth, n, H_D, hw), F32)
    if kv_prev is None:
        kv_spec = pl.BlockSpec((depth, tm, H_D, hw), lambda i: (0, i, 0, 0))
        slot = layer
    else:
        kv_spec = pl.BlockSpec((1, tm, H_D, hw), lambda i: (layer, i, 0, 0))
        slot = 0
    return pl.pallas_call(
        functools.partial(_inproj_kernel, n_alias=n_alias, slot=slot),
        grid=(n // tm,),
        in_specs=[
            pl.BlockSpec((tm, D_MODEL), lambda i: (i, 0)),
            _layer_spec((1, D_MODEL), layer),
            _layer_spec((D_MODEL, WP_W), layer),
            _layer_spec((LANES, LANES), layer),
            _layer_spec((1, LANES), layer),
        ] + [pl.BlockSpec(memory_space=pl.ANY)] * n_alias,
        out_specs=[
            pl.BlockSpec((tm, Z_W), lambda i: (i, 0)),
            kv_spec, kv_spec,
            pl.BlockSpec((tm, W_DIFF), lambda i: (i, 0)),
            pl.BlockSpec((tm, W_DIFF), lambda i: (i, 0)),
        ],
        out_shape=[
            jax.ShapeDtypeStruct((n, Z_W), F32),
            kv_shape, kv_shape,
            jax.ShapeDtypeStruct((n, W_DIFF), BF16),
            jax.ShapeDtypeStruct((n, W_DIFF), BF16),
        ],
        input_output_aliases={} if kv_prev is None else {5: 1, 6: 2},
        scratch_shapes=[pltpu.VMEM((tm, D_MODEL), BF16)],
        compiler_params=_cparams(("parallel",), 56),
        name="inproj",
    )(x, g, w, wa2, ba, *(() if kv_prev is None else kv_prev))


def _gla_kernel(q_ref, k_ref, v_ref, la_ref, s0_ref, o_ref, st_ref, st_scr, *, n_chunks):
    c = CHUNK

    @pl.when(pl.program_id(1) == 0)
    def _():
        st_scr[...] = s0_ref[0]

    r_i = lax.broadcasted_iota(jnp.int32, (c, c), 0)
    c_i = lax.broadcasted_iota(jnp.int32, (c, c), 1)
    tri = (r_i >= c_i).astype(BF16)
    head_k = lax.broadcasted_iota(jnp.int32, (H_A * DK_A, W_GLA), 0) // DK_A
    head_v = lax.broadcasted_iota(jnp.int32, (H_A * DK_A, W_GLA), 1) // DV_A
    sum_bcast = (head_k == head_v).astype(BF16)
    st_hv = lax.broadcasted_iota(jnp.int32, (W_GLA, H_A * DK_A), 0) // DV_A
    st_hk = lax.broadcasted_iota(jnp.int32, (W_GLA, H_A * DK_A), 1) // DK_A
    st_mask = st_hv == st_hk
    sub = lax.broadcasted_iota(jnp.int32, (SUBLANES, H_A * DK_A), 0)
    n_groups = c // SUBLANES
    sub_groups = GLA_SUB // SUBLANES
    head_q = lax.broadcasted_iota(jnp.int32, (GLA_SUB, H_A * DK_A), 1) // DK_A
    head_o = lax.broadcasted_iota(jnp.int32, (GLA_SUB, W_GLA), 1) // DV_A

    n_sub = c // GLA_SUB

    def front(ci):
        rows = slice(ci * c, (ci + 1) * c)
        q = q_ref[0, rows, :] * (DK_A ** -0.5)
        k = k_ref[0, rows, :]
        la = la_ref[0, rows, :]

        la1 = la.astype(BF16)
        rem = la - la1.astype(F32)
        la2 = rem.astype(BF16)
        la3 = (rem - la2.astype(F32)).astype(BF16)
        b3 = jnp.dot(tri, jnp.concatenate([la1, la2, la3], axis=1), preferred_element_type=F32)
        b = b3[:, :LANES] + b3[:, LANES:2 * LANES] + b3[:, 2 * LANES:]

        pieces = []
        for s in range(c):
            blk_end = (s // GLA_SUB + 1) * GLA_SUB
            r0 = (s // SUBLANES) * SUBLANES
            n_r = blk_end - r0
            e = jnp.exp(jnp.minimum(b[r0:blk_end, :] - b[s:s + 1, :], 0.0))
            p = q[r0:blk_end, :] * e * k[s:s + 1, :]
            if s % SUBLANES:
                first = jnp.where(sub >= (s % SUBLANES), p[:SUBLANES], 0.0)
                p = first if n_r == SUBLANES else jnp.concatenate([first, p[SUBLANES:]], axis=0)
            pieces.append(p)
        p_all = jnp.concatenate(pieces, axis=0).astype(BF16)
        w_all = jnp.dot(p_all, sum_bcast, preferred_element_type=F32)
        a_blocks = []
        for i in range(1, n_sub):
            lo = i * GLA_SUB
            ref = b[lo - 1:lo, :]
            q_i = q[lo:lo + GLA_SUB, :] * jnp.exp(b[lo:lo + GLA_SUB, :] - ref)
            k_i = (k[:lo, :] * jnp.exp(ref - b[:lo, :])).astype(BF16)
            q_heads = jnp.concatenate(
                [jnp.where(head_q == hh, q_i, 0.0) for hh in range(H_A)], axis=0).astype(BF16)
            a_blocks.append(lax.dot_general(q_heads, k_i, (((1,), (1,)), ((), ())),
                                            preferred_element_type=F32).astype(BF16))
        b_last = b[c - 1:c, :]
        return dict(w_all=w_all, a_blocks=a_blocks, q_dec=(q * jnp.exp(b)).astype(BF16),
                    k_dec=(k * jnp.exp(b_last - b)).astype(BF16), decay=jnp.exp(b_last))

    def middle(ci, f):
        v_bf = v_ref[0, ci * c:(ci + 1) * c, :].astype(BF16)
        f['r_blocks'] = [jnp.dot(a, v_bf[:(i + 1) * GLA_SUB, :], preferred_element_type=F32)
                         for i, a in enumerate(f.pop('a_blocks'))]
        f['upd'] = lax.dot_general(v_bf, f.pop('k_dec'), (((0,), (0,)), ((), ())), preferred_element_type=F32)
        return f

    def back(ci, f, st):
        v = v_ref[0, ci * c:(ci + 1) * c, :]
        w_all = f['w_all']
        acc = [None] * n_groups
        off = 0
        for s in range(c):
            v_s = v[s:s + 1, :]
            for g in range(s // SUBLANES, (s // GLA_SUB + 1) * GLA_SUB // SUBLANES):
                t = w_all[off:off + SUBLANES, :] * v_s
                acc[g] = t if acc[g] is None else acc[g] + t
                off += SUBLANES
        o_blocks = [jnp.concatenate(acc[i * sub_groups:(i + 1) * sub_groups], axis=0) for i in range(n_sub)]
        for i, r_i in enumerate(f['r_blocks']):
            for hh in range(H_A):
                o_blocks[i + 1] = o_blocks[i + 1] + jnp.where(
                    head_o == hh, r_i[hh * GLA_SUB:(hh + 1) * GLA_SUB], 0.0)
        o_inter = lax.dot_general(f['q_dec'], st.astype(BF16), (((1,), (1,)), ((), ())),
                                  preferred_element_type=F32)
        st_new = st * f['decay'] + jnp.where(st_mask, f['upd'], 0.0)
        return st_new, jnp.concatenate(o_blocks, axis=0) + o_inter

    st = st_scr[...]
    outs = []
    stage = {}
    for t in range(n_chunks + 2):
        if t < n_chunks:
            stage[t] = front(t)
        if 0 <= t - 1 < n_chunks:
            stage[t - 1] = middle(t - 1, stage[t - 1])
        if 0 <= t - 2 < n_chunks:
            st, o = back(t - 2, stage.pop(t - 2), st)
            outs.append(o)
    for ci in range(n_chunks):
        o_ref[0, ci * c:(ci + 1) * c, :] = outs[ci]
    st_scr[...] = st
    st_ref[0] = st


def _gla(z, s0t):
    b, t, _ = z.shape
    tb = min(SCAN_TILE, t)
    kw = H_A * DK_A
    return pl.pallas_call(
        functools.partial(_gla_kernel, n_chunks=tb // CHUNK),
        grid=(b, t // tb),
        in_specs=[
            pl.BlockSpec((1, tb, kw), lambda i, j: (i, j, Z_GQ // kw)),
            pl.BlockSpec((1, tb, kw), lambda i, j: (i, j, Z_GK // kw)),
            pl.BlockSpec((1, tb, W_GLA), lambda i, j: (i, j, Z_GV // W_GLA)),
            pl.BlockSpec((1, tb, kw), lambda i, j: (i, j, Z_LA // kw)),
            pl.BlockSpec((1, W_GLA, kw), lambda i, j: (i, 0, 0)),
        ],
        out_specs=[
            pl.BlockSpec((1, tb, W_GLA), lambda i, j: (i, j, 0)),
            pl.BlockSpec((1, W_GLA, kw), lambda i, j: (i, 0, 0)),
        ],
        out_shape=[
            jax.ShapeDtypeStruct((b, t, W_GLA), F32),
            jax.ShapeDtypeStruct((b, W_GLA, kw), F32),
        ],
        scratch_shapes=[pltpu.VMEM((W_GLA, kw), F32)],
        compiler_params=_cparams(("parallel", "arbitrary"), 32),
        name="gla",
    )(z, z, z, z, s0t)


def _swap_halves(x):
    lane = lax.broadcasted_iota(jnp.int32, (x.shape[0], LANES), 1)
    first_half = (lane % DK_R) < (DK_R // 2)
    cols = []
    for lo in range(0, x.shape[1], LANES):
        blk = x[:, lo:lo + LANES]
        up = pltpu.roll(blk, LANES - DK_R // 2, axis=1)
        down = pltpu.roll(blk, DK_R // 2, axis=1)
        cols.append(jnp.where(first_half, up, down))
    return jnp.concatenate(cols, axis=1)


def _ret_kernel(q_ref, k_ref, v_ref, cos_ref, sin_ref, dec_ref, qdec_ref, kdec_ref, cdec_ref, s0_ref,
                o_ref, st_ref, st_scr, *, n_chunks, c):
    @pl.when(pl.program_id(1) == 0)
    def _():
        st_scr[...] = s0_ref[0]

    head_l = lax.broadcasted_iota(jnp.int32, (c, W_RET), 1) // DK_R
    st_mask = (lax.broadcasted_iota(jnp.int32, (W_RET, W_RET), 0) // DV_R
               == lax.broadcasted_iota(jnp.int32, (W_RET, W_RET), 1) // DK_R)

    def front(ci):
        rows = slice(ci * c, (ci + 1) * c)
        cos = cos_ref[rows, :]
        sin = sin_ref[rows, :]
        q = q_ref[0, rows, :]
        k = k_ref[0, rows, :]
        q = q * cos + _swap_halves(q) * sin
        k = (k * cos + _swap_halves(k) * sin) * (DK_R ** -0.5)
        q_stack = jnp.concatenate(
            [jnp.where(head_l == h, q, 0.0) for h in range(H_R)], axis=0).astype(BF16)
        s = lax.dot_general(q_stack, k.astype(BF16), (((1,), (1,)), ((), ())),
                            preferred_element_type=F32)
        return dict(s=s, q_dec=(q * qdec_ref[...]).astype(BF16), k_dec=(k * kdec_ref[...]).astype(BF16))

    def middle(ci, f):
        v = v_ref[0, ci * c:(ci + 1) * c, :].astype(BF16)
        s = (f.pop('s') * dec_ref[...]).astype(BF16)
        f['r'] = jnp.dot(s, v, preferred_element_type=F32)
        f['upd'] = lax.dot_general(v, f.pop('k_dec'), (((0,), (0,)), ((), ())), preferred_element_type=F32)
        return f

    def back(f, st):
        r = f['r']
        o = jnp.where(head_l == 0, r[:c], 0.0)
        for h in range(1, H_R):
            o = o + jnp.where(head_l == h, r[h * c:(h + 1) * c], 0.0)
        o = o + lax.dot_general(f['q_dec'], st.astype(BF16), (((1,), (1,)), ((), ())),
                                preferred_element_type=F32)
        return st * cdec_ref[...] + jnp.where(st_mask, f['upd'], 0.0), o

    st = st_scr[...]
    outs = []
    stage = {}
    for t in range(n_chunks + 2):
        if t < n_chunks:
            stage[t] = front(t)
        if 0 <= t - 1 < n_chunks:
            stage[t - 1] = middle(t - 1, stage[t - 1])
        if 0 <= t - 2 < n_chunks:
            st, o = back(stage.pop(t - 2), st)
            outs.append(o)
    for ci in range(n_chunks):
        o_ref[0, ci * c:(ci + 1) * c, :] = outs[ci]
    st_scr[...] = st
    st_ref[0] = st


def _ret_tables(c):
    lg = jnp.log(1.0 - 2.0 ** (-5.0 - jnp.arange(H_R, dtype=F32)))
    idx = jnp.arange(c, dtype=F32)
    tri = idx[:, None] >= idx[None, :]
    dec = jnp.exp(jnp.where(tri[None], (idx[:, None] - idx[None, :])[None] * lg[:, None, None], -jnp.inf))
    q_dec = jnp.exp((idx[None, :] + 1.0) * lg[:, None])
    k_dec = jnp.exp((c - 1.0 - idx[None, :]) * lg[:, None])
    c_dec = jnp.exp(c * lg)
    return (dec.reshape(H_R * c, c),
            jnp.repeat(q_dec.T, DK_R, axis=1),
            jnp.repeat(k_dec.T, DK_R, axis=1),
            jnp.repeat(c_dec, DK_R)[None, :])


def _rope_tables(pos):
    half = DK_R // 2
    inv = ROPE_BASE ** (-jnp.arange(half, dtype=F32) / half)
    ang = pos.astype(F32)[:, None] * inv[None, :]
    cos = jnp.cos(ang)
    sin = jnp.sin(ang)
    cos_t = jnp.tile(jnp.concatenate([cos, cos], axis=1), (1, H_R))
    sin_t = jnp.tile(jnp.concatenate([-sin, sin], axis=1), (1, H_R))
    return cos_t, sin_t


def _ret(z, s0t, past_len):
    b, t, _ = z.shape
    tb = min(SCAN_TILE, t)
    c = min(RET_CHUNK, t)
    cos_t, sin_t = _rope_tables(past_len + jnp.arange(t))
    dec, q_dec, k_dec, c_dec = _ret_tables(c)
    return pl.pallas_call(
        functools.partial(_ret_kernel, n_chunks=tb // c, c=c),
        grid=(b, t // tb),
        in_specs=[
            pl.BlockSpec((1, tb, W_RET), lambda i, j: (i, j, Z_RQ // W_RET)),
            pl.BlockSpec((1, tb, W_RET), lambda i, j: (i, j, Z_RK // W_RET)),
            pl.BlockSpec((1, tb, W_RET), lambda i, j: (i, j, Z_RV // W_RET)),
            pl.BlockSpec((tb, W_RET), lambda i, j: (j, 0)),
            pl.BlockSpec((tb, W_RET), lambda i, j: (j, 0)),
            _const_spec((H_R * c, c)),
            _const_spec((c, W_RET)),
            _const_spec((c, W_RET)),
            _const_spec((1, W_RET)),
            pl.BlockSpec((1, W_RET, W_RET), lambda i, j: (i, 0, 0)),
        ],
        out_specs=[
            pl.BlockSpec((1, tb, W_RET), lambda i, j: (i, j, 0)),
            pl.BlockSpec((1, W_RET, W_RET), lambda i, j: (i, 0, 0)),
        ],
        out_shape=[
            jax.ShapeDtypeStruct((b, t, W_RET), F32),
            jax.ShapeDtypeStruct((b, W_RET, W_RET), F32),
        ],
        scratch_shapes=[pltpu.VMEM((W_RET, W_RET), F32)],
        compiler_params=_cparams(("parallel", "arbitrary"), 32),
        name="ret",
    )(z, z, z, cos_t, sin_t, dec, q_dec, k_dec, c_dec, s0t)


V_ROWS = DV_D + 16


LOG2E = 1.4426950408889634


def _stack_maps(q):
    lane = lax.broadcasted_iota(jnp.int32, q.shape, 1)
    q = q * (DK_D ** -0.5 * LOG2E)
    return jnp.concatenate(
        [jnp.where(lane < DK_D, q, 0.0), jnp.where(lane >= DK_D, q, 0.0)], axis=0).astype(BF16)


def _scores_t(k_tile, q_stack):
    return lax.dot_general(k_tile, q_stack, (((1,), (1,)), ((), ())), preferred_element_type=F32)


def _softmax_t(s, rel, shift, m_old):
    s = s + rel
    m_new = jnp.maximum(m_old, jnp.max(s, axis=0, keepdims=True) + shift)
    p = jnp.exp2(s - (m_new - shift)).astype(BF16)
    return m_new, p, jnp.exp2(m_old - m_new)


def _accumulate_t(acc, alpha, vt_tile, p):
    return alpha * acc + jnp.dot(vt_tile, p, preferred_element_type=F32)


def _attn_finish(acc, lam, g, out_scale, tq):
    o_t = acc[0:DV_D] / acc[DV_D:DV_D + 1]
    o = o_t[:, :tq] - lam * o_t[:, tq:]
    y_t = o * lax.rsqrt(jnp.mean(o * o, axis=0, keepdims=True) + EPS)
    return y_t.T * g * out_scale


def _with_ones(v_t):
    return jnp.concatenate([v_t, jnp.ones((V_ROWS - DV_D, v_t.shape[1]), BF16)], axis=0)


def _attn_prompt_kernel(lam_ref, slope_ref, q_ref, kb_ref, vb_ref, relc_ref, reld_ref, g_ref, o_ref,
                        vt_scr, s_a, s_b, p_a, p_b, acc_scr, *, tq, n_tiles, layer, out_scale):
    h = pl.program_id(1)
    for i in range(n_tiles):
        v_t = vb_ref[0, i * tq:(i + 1) * tq, :].astype(F32).T.astype(BF16)
        vt_scr[i] = _with_ones(v_t)
    slope2 = slope_ref[h] * LOG2E
    s_bufs = (s_a, s_b)
    p_bufs = (p_a, p_b)
    col_groups = range(0, 2 * tq, LANES)

    def softmax_all(s_ref, rel_ref, shift, m_old):
        return [_softmax_t(s_ref[:, lo:lo + LANES], rel_ref[0, :, lo:lo + LANES], shift, m_old[i])
                for i, lo in enumerate(col_groups)]

    for qi in range(n_tiles):
        q_stack = _stack_maps(q_ref[0, qi * tq:(qi + 1) * tq, :])

        def scores(j):
            return _scores_t(kb_ref[0, j * tq:(j + 1) * tq, :], q_stack)

        s_bufs[0][...] = scores(0)
        m = [jnp.full((1, LANES), NEG_INF, F32) for _ in col_groups]
        alpha_prev = None
        for j in range(qi):
            s_in, s_out = s_bufs[j % 2], s_bufs[1 - j % 2]
            p_in, p_out = p_bufs[1 - j % 2], p_bufs[j % 2]
            stats = softmax_all(s_in, relc_ref, -slope2 * float((qi - j) * tq), m)
            s_next = scores(j + 1)
            if j == 1:
                acc_new = jnp.dot(vt_scr[0], p_in[...], preferred_element_type=F32)
            elif j > 1:
                acc_new = _accumulate_t(acc_scr[...], alpha_prev, vt_scr[j - 1], p_in[...])
            for (m_new, p, alpha), lo in zip(stats, col_groups):
                p_out[:, lo:lo + LANES] = p
            m = [st[0] for st in stats]
            alpha_prev = jnp.concatenate([st[2] for st in stats], axis=1)
            s_out[...] = s_next
            if j >= 1:
                acc_scr[...] = acc_new
        stats = softmax_all(s_bufs[qi % 2], reld_ref, 0.0, m)
        p = jnp.concatenate([st[1] for st in stats], axis=1)
        alpha = jnp.concatenate([st[2] for st in stats], axis=1)
        if qi == 0:
            acc = jnp.dot(vt_scr[0], p, preferred_element_type=F32)
        else:
            if qi == 1:
                acc = jnp.dot(vt_scr[0], p_bufs[1 - qi % 2][...], preferred_element_type=F32)
            else:
                acc = _accumulate_t(acc_scr[...], alpha_prev, vt_scr[qi - 1], p_bufs[1 - qi % 2][...])
            acc = _accumulate_t(acc, alpha, vt_scr[qi], p)
        o_ref[0, qi * tq:(qi + 1) * tq, :] = _attn_finish(acc, lam_ref[layer], g_ref[...], out_scale, tq)


def _attn_tables(tk, tq):
    slopes = jnp.asarray([2.0 ** (-8.0 * (i + 1) / H_D) for i in range(H_D)], F32)
    slopes2 = slopes * LOG2E
    key = lax.broadcasted_iota(jnp.int32, (tk, 2 * tq), 0)
    qry = lax.broadcasted_iota(jnp.int32, (tk, 2 * tq), 1) % tq
    rel_ctx = slopes2[:, None, None] * (key - qry).astype(F32)[None]
    key_d = lax.broadcasted_iota(jnp.int32, (tq, 2 * tq), 0)
    qry_d = lax.broadcasted_iota(jnp.int32, (tq, 2 * tq), 1) % tq
    visible = (key_d // CHUNK) <= (qry_d // CHUNK)
    rel_diag = jnp.where(visible[None], -slopes2[:, None, None] * jnp.abs(qry_d - key_d).astype(F32)[None],
                         NEG_INF)
    return slopes, rel_ctx, rel_diag


def _attn_prompt(lam, z3, kb, vb, g, layer, out_scale):
    b, t, _ = z3.shape
    tq = min(ATTN_TILE, t)
    n_tiles = t // tq
    hw = 2 * DK_D
    slopes, rel_ctx, rel_diag = _attn_tables(tq, tq)
    smem = pl.BlockSpec(memory_space=pltpu.SMEM)
    return pl.pallas_call(
        functools.partial(_attn_prompt_kernel, tq=tq, n_tiles=n_tiles, layer=layer, out_scale=out_scale),
        grid=(b, H_D),
        in_specs=[
            smem, smem,
            pl.BlockSpec((1, t, hw), lambda i, h: (i, 0, Z_DQ // hw + h)),
            pl.BlockSpec((1, t, hw), lambda i, h: (i, 0, h)),
            pl.BlockSpec((1, t, DV_D), lambda i, h: (i, 0, h)),
            pl.BlockSpec((1, tq, 2 * tq), lambda i, h: (h, 0, 0)),
            pl.BlockSpec((1, tq, 2 * tq), lambda i, h: (h, 0, 0)),
            _layer_spec((1, DV_D), layer),
        ],
        out_specs=pl.BlockSpec((1, t, DV_D), lambda i, h: (i, 0, h)),
        out_shape=jax.ShapeDtypeStruct((b, t, W_DIFF), F32),
        scratch_shapes=[
            pltpu.VMEM((n_tiles, V_ROWS, tq), BF16),
            pltpu.VMEM((tq, 2 * tq), F32), pltpu.VMEM((tq, 2 * tq), F32),
            pltpu.VMEM((tq, 2 * tq), BF16), pltpu.VMEM((tq, 2 * tq), BF16),
            pltpu.VMEM((V_ROWS, 2 * tq), F32),
        ],
        compiler_params=_cparams(("parallel", "parallel"), 32),
        name="attn_prompt",
    )(lam, slopes, z3, kb, vb, rel_ctx, rel_diag, g)


def _attn_sample_kernel(lam_ref, slope_ref, q01_ref, q23_ref, kc_hbm, vc_hbm, kb_ref, vb_ref, relc_ref, reld_ref,
                        g_ref, o_ref, k_buf, v_buf, sem, *, tq, tk, n_ctx, q_off, layer, out_scale):
    hw = 2 * DK_D
    row = pl.program_id(0)

    def head_copies(h, slot):
        return (pltpu.make_async_copy(kc_hbm.at[layer, row, :, h, :], k_buf.at[slot], sem.at[0, slot]),
                pltpu.make_async_copy(vc_hbm.at[layer, row, :, h, :], v_buf.at[slot], sem.at[1, slot]))

    for cp in head_copies(0, 0):
        cp.start()
    for h in range(H_D):
        slot = h % 2
        if h + 1 < H_D:
            for cp in head_copies(h + 1, 1 - slot):
                cp.start()
        for cp in head_copies(h, slot):
            cp.wait()
        slope2 = slope_ref[h] * LOG2E
        q_ref = q01_ref if h < 2 else q23_ref
        q_stack = _stack_maps(q_ref[0, :, (h % 2) * hw:(h % 2 + 1) * hw])
        m = jnp.full((1, 2 * tq), NEG_INF, F32)
        acc = jnp.zeros((V_ROWS, 2 * tq), F32)
        for j in range(n_ctx):
            k_tile = k_buf[slot, j * tk:(j + 1) * tk, :].astype(BF16)
            v_t = v_buf[slot, j * tk:(j + 1) * tk, :].T.astype(BF16)
            shift = -slope2 * float(q_off - j * tk)
            m, p, alpha = _softmax_t(_scores_t(k_tile, q_stack), relc_ref[h], shift, m)
            acc = _accumulate_t(acc, alpha, _with_ones(v_t), p)
        v_t = vb_ref[0, :, h * hw:(h + 1) * hw].astype(F32).T.astype(BF16)
        s = _scores_t(kb_ref[0, :, h * hw:(h + 1) * hw], q_stack)
        _, p, alpha = _softmax_t(s, reld_ref[h], 0.0, m)
        acc = _accumulate_t(acc, alpha, _with_ones(v_t), p)
        o_ref[0, :, h * hw:(h + 1) * hw] = _attn_finish(acc, lam_ref[layer], g_ref[...], out_scale, tq)


def _attn_sample(lam, z3, cache_k, cache_v, layer, kb, vb, g, out_scale):
    b, t, _ = z3.shape
    t_past = cache_k.shape[2]
    tk = min(ATTN_CTX_TILE, t_past)
    slopes, rel_ctx, rel_diag = _attn_tables(tk, t)
    smem = pl.BlockSpec(memory_space=pltpu.SMEM)
    return pl.pallas_call(
        functools.partial(_attn_sample_kernel, tq=t, tk=tk, n_ctx=t_past // tk, q_off=t_past, layer=layer,
                          out_scale=out_scale),
        grid=(b,),
        in_specs=[
            smem, smem,
            pl.BlockSpec((1, t, W_DIFF // 2), lambda i: (i, 0, Z_DQ // (W_DIFF // 2))),
            pl.BlockSpec((1, t, W_DIFF // 2), lambda i: (i, 0, Z_DQ // (W_DIFF // 2) + 1)),
            pl.BlockSpec(memory_space=pl.ANY),
            pl.BlockSpec(memory_space=pl.ANY),
            pl.BlockSpec((1, t, W_DIFF), lambda i: (i, 0, 0)),
            pl.BlockSpec((1, t, W_DIFF), lambda i: (i, 0, 0)),
            _const_spec((H_D, tk, 2 * t)),
            _const_spec((H_D, t, 2 * t)),
            _layer_spec((1, DV_D), layer),
        ],
        out_specs=pl.BlockSpec((1, t, W_DIFF), lambda i: (i, 0, 0)),
        out_shape=jax.ShapeDtypeStruct((b, t, W_DIFF), F32),
        scratch_shapes=[
            pltpu.VMEM((2, t_past, 2 * DK_D), F32),
            pltpu.VMEM((2, t_past, DV_D), F32),
            pltpu.SemaphoreType.DMA((2, 2)),
        ],
        compiler_params=_cparams(("parallel",), 32),
        name="attn_sample",
    )(lam, slopes, z3, z3, cache_k, cache_v, kb, vb, rel_ctx, rel_diag, g)


def _seg_mean(x, avg):
    hi = x.astype(BF16)
    lo = (x - hi.astype(F32)).astype(BF16)
    return jnp.dot(jnp.concatenate([hi, lo], axis=1), avg, preferred_element_type=F32)


def _outproj_ffn_kernel(x_ref, oa_ref, gr_ref, od_ref, or_ref, rg_ref, ga_ref, gret_ref, w_ref, gmix_ref,
                        gpre_ref, wup_ref, wdown_ref, gpost_ref, o_ref, h_scr, act_scr):
    lane_r = lax.broadcasted_iota(jnp.int32, (2 * W_GLA, W_GLA), 0) % W_GLA // DV_A
    lane_c = lax.broadcasted_iota(jnp.int32, (2 * W_GLA, W_GLA), 1) // DV_A
    avg = jnp.where(lane_r == lane_c, 1.0 / DV_A, 0.0).astype(BF16)

    oa = oa_ref[...]
    gr = gr_ref[...]
    oa_n = oa * lax.rsqrt(_seg_mean(oa * oa, avg) + EPS) * ga_ref[...]
    h_scr[:, 0:W_GLA] = (oa_n * (gr * _sigmoid(gr))).astype(BF16)

    h_scr[:, W_GLA:W_GLA + W_DIFF] = od_ref[...].astype(BF16)

    orr = or_ref[...]
    rg = rg_ref[...]
    cen = orr - _seg_mean(orr, avg)
    or_n = cen * lax.rsqrt(_seg_mean(cen * cen, avg) + EPS) * gret_ref[...]
    h_scr[:, W_GLA + W_DIFF:] = (or_n * (rg * _sigmoid(rg))).astype(BF16)

    m = jnp.dot(h_scr[...], w_ref[...], preferred_element_type=F32)
    x = x_ref[...] + _rms(m, gmix_ref[...])
    o_ref[...] = _ffn_math(x, gpre_ref, wup_ref, wdown_ref, gpost_ref, h_scr, act_scr)


def _outproj_ffn(x, oa, z, od, orr, ga, gret, w, gmix, gpre, wup, wdown, gpost, layer):
    n = x.shape[0]
    tm = min(TOKEN_TILE, n)
    return pl.pallas_call(
        _outproj_ffn_kernel,
        grid=(n // tm,),
        in_specs=[
            pl.BlockSpec((tm, D_MODEL), lambda i: (i, 0)),
            pl.BlockSpec((tm, W_GLA), lambda i: (i, 0)),
            pl.BlockSpec((tm, W_GLA), lambda i: (i, Z_GR // W_GLA)),
            pl.BlockSpec((tm, W_DIFF), lambda i: (i, 0)),
            pl.BlockSpec((tm, W_RET), lambda i: (i, 0)),
            pl.BlockSpec((tm, W_RET), lambda i: (i, Z_RG // W_RET)),
            _layer_spec((1, W_GLA), layer),
            _layer_spec((1, W_RET), layer),
            _layer_spec((D_MODEL, D_MODEL), layer),
            _layer_spec((1, D_MODEL), layer),
            _layer_spec((1, D_MODEL), layer),
            _layer_spec((D_MODEL, 2 * D_FF), layer),
            _layer_spec((D_FF, D_MODEL), layer),
            _layer_spec((1, D_MODEL), layer),
        ],
        out_specs=pl.BlockSpec((tm, D_MODEL), lambda i: (i, 0)),
        out_shape=jax.ShapeDtypeStruct((n, D_MODEL), F32),
        scratch_shapes=[pltpu.VMEM((tm, D_MODEL), BF16), pltpu.VMEM((tm, D_FF), BF16)],
        compiler_params=_cparams(("parallel",), 52),
        name="outproj_ffn",
    )(x, oa, z, od, orr, z, ga, gret, w, gmix, gpre, wup, wdown, gpost)


def _regroup_w_in(w):
    pad = jnp.zeros(w.shape[:-1] + (LANES - GLA_RANK,), w.dtype)
    return jnp.concatenate([w[..., GLA_RANK:], w[..., :GLA_RANK], pad], axis=-1).astype(BF16)


def _state_to_blockdiag_t(s):
    b, h, dk, dv = s.shape
    eye = jnp.eye(h, dtype=s.dtype)
    return jnp.einsum('bhdv,hg->bhvgd', s, eye).reshape(b, h * dv, h * dk)


def _blockdiag_t_to_state(st, h, dk, dv):
    b = st.shape[0]
    blocks = st.reshape(b, h, dv, h, dk)
    eye = jnp.eye(h, dtype=st.dtype)
    diag = jnp.sum(blocks * eye[None, :, None, :, None], axis=3)
    return diag.swapaxes(-1, -2)


def _layer(x, l, depth, p, past_k, past_v, s_gla0, s_ret0, kv_prev):
    b, t, _ = x.shape
    n = b * t
    x = _ffn(x.reshape(n, D_MODEL), p['ffn1_norm_pre'], p['ffn1_w_up'], p['ffn1_w_down'], p['ffn1_norm_post'], l)
    z, k4, v4, kb, vb = _inproj(x, p['mix_norm_pre'], p['w_in'], p['w_gla_a2'], p['b_gla_a'], l, depth, kv_prev)
    z3 = z.reshape(b, t, Z_W)
    kb3 = kb.reshape(b, t, W_DIFF)
    vb3 = vb.reshape(b, t, W_DIFF)

    o_a, st_gla = _gla(z3, s_gla0)

    out_scale = 1.0 - (0.8 - 0.6 * math.exp(-0.3 * l))
    if past_k is None:
        o_d = _attn_prompt(p['lam'], z3, kb3, vb3, p['diff_subln'], l, out_scale)
        past_len = 0
    else:
        o_d = _attn_sample(p['lam'], z3, past_k, past_v, l, kb3, vb3, p['diff_subln'], out_scale)
        past_len = past_k.shape[2]

    o_r, st_ret = _ret(z3, s_ret0, past_len)

    x = _outproj_ffn(x, o_a.reshape(n, W_GLA), z, o_d.reshape(n, W_DIFF), o_r.reshape(n, W_RET),
                     p['gla_norm'], p['ret_norm'], p['w_out'], p['mix_norm_post'],
                     p['ffn2_norm_pre'], p['ffn2_w_up'], p['ffn2_w_down'], p['ffn2_norm_post'], l)
    return x.reshape(b, t, D_MODEL), (k4, v4), st_gla, st_ret


def kernel(x_prompt, x_sample, cache_diff_k, cache_diff_v, state_gla, state_ret, ffn1_norm_pre, ffn1_w_up, ffn1_w_down, ffn1_norm_post, mix_norm_pre, w_in, w_gla_a2, b_gla_a, gla_norm, diff_lambda, diff_subln, ret_norm, w_out, mix_norm_post, ffn2_norm_pre, ffn2_w_up, ffn2_w_down, ffn2_norm_post):
    depth = w_in.shape[0]
    row = lambda a: a.reshape(depth, 1, -1)
    lam_p = diff_lambda.astype(F32)
    lam_init = jnp.asarray([0.8 - 0.6 * math.exp(-0.3 * l) for l in range(depth)], F32)
    lam = (jnp.exp(jnp.sum(lam_p[:, 0] * lam_p[:, 1], axis=-1))
           - jnp.exp(jnp.sum(lam_p[:, 2] * lam_p[:, 3], axis=-1)) + lam_init)
    p = {
        'ffn1_norm_pre': row(ffn1_norm_pre), 'ffn1_w_up': ffn1_w_up.astype(BF16),
        'ffn1_w_down': ffn1_w_down.astype(BF16), 'ffn1_norm_post': row(ffn1_norm_post),
        'mix_norm_pre': row(mix_norm_pre),
        'w_in': _regroup_w_in(w_in),
        'w_gla_a2': jnp.pad(w_gla_a2, ((0, 0), (0, LANES - GLA_RANK), (0, 0))).astype(BF16),
        'b_gla_a': row(b_gla_a), 'gla_norm': row(gla_norm), 'lam': lam,
        'diff_subln': row(diff_subln), 'ret_norm': row(ret_norm), 'w_out': w_out.astype(BF16),
        'mix_norm_post': row(mix_norm_post),
        'ffn2_norm_pre': row(ffn2_norm_pre), 'ffn2_w_up': ffn2_w_up.astype(BF16),
        'ffn2_w_down': ffn2_w_down.astype(BF16), 'ffn2_norm_post': row(ffn2_norm_post),
    }
    bp = x_prompt.shape[0]
    bs = x_sample.shape[0]
    zeros_gla = jnp.zeros((bp, W_GLA, H_A * DK_A), F32)
    zeros_ret = jnp.zeros((bp, W_RET, H_R * DK_R), F32)
    xp, xs = x_prompt, x_sample
    kv_p = kv_s = None
    st_p, st_s = [], []
    for l in range(depth):
        xp, kv_p, g_, r_ = _layer(xp, l, depth, p, None, None, zeros_gla, zeros_ret, kv_p)
        st_p.append((g_, r_))
        xs, kv_s, g_, r_ = _layer(xs, l, depth, p, cache_diff_k, cache_diff_v,
                                  _state_to_blockdiag_t(state_gla[l]), _state_to_blockdiag_t(state_ret[l]), kv_s)
        st_s.append((g_, r_))

    def states(sts):
        g = jnp.stack([_blockdiag_t_to_state(s[0], H_A, DK_A, DV_A) for s in sts])
        r = jnp.stack([_blockdiag_t_to_state(s[1], H_R, DK_R, DV_R) for s in sts])
        return g, r

    gp, rp = states(st_p)
    gs, rs = states(st_s)
    kv4 = lambda a, b: a.reshape(depth, b, -1, H_D, 2 * DK_D)
    return (xp, xs, kv4(kv_p[0], bp), kv4(kv_p[1], bp), gp, rp, kv4(kv_s[0], bs), kv4(kv_s[1], bs), gs, rs)
```
